```python
import jax, jax.numpy as jnp
from jax import lax
import numpy as np

D_MODEL = 1024
BATCH = 16
SEQ = 4096
DEPTH = 4

CHUNK = 64
MEM_LEN = 256
HGRN_HEADS = 4
HGRN_HEAD_DIM = 128
HGRN_WIDTH = HGRN_HEADS * HGRN_HEAD_DIM
ATTN_HEADS = 8
ATTN_HEAD_DIM = 64
ATTN_WIDTH = ATTN_HEADS * ATTN_HEAD_DIM
LEFT_CHUNKS = 8
BAND = (LEFT_CHUNKS + 1) * CHUNK
MAX_REL = 128
N_REL = 2 * MAX_REL + 1
MIX_WIDTH = HGRN_WIDTH + ATTN_WIDTH
IN_WIDTH = 4 * HGRN_WIDTH + 3 * ATTN_WIDTH
MEM_HEADS = 4
MEM_HEAD_DIM = D_MODEL // MEM_HEADS
D_FF = 2816
N_EXPERTS = 8
TOP_K = 2
D_FF_EXPERT = 3584
MOE_BLOCK = 512
N_DENSE = (DEPTH + 1) // 2
N_MOE = DEPTH // 2
EPS = 1e-6

kernel_name = "hybrid_hgrn2_chunkattn_moe_encoder"


def rms_norm(x, gain):
    xf = x.astype(jnp.float32)
    y = xf * lax.rsqrt(jnp.mean(xf * xf, axis=-1, keepdims=True) + EPS)
    return (y * gain.astype(jnp.float32)).astype(x.dtype)


def hgrn2_mixer(q, f_logit, i_in, g, lower_bound, out_gain):
    B, S, _ = q.shape
    n_chunks = S // CHUNK
    lb = lower_bound.astype(jnp.float32)
    f = lb + (1.0 - lb) * jax.nn.sigmoid(f_logit.astype(jnp.float32))
    log_f = jnp.log(f)
    k = 1.0 - f

    def to_chunks(t):
        return t.reshape(B, n_chunks, CHUNK, HGRN_HEADS, HGRN_HEAD_DIM).transpose(1, 0, 3, 2, 4)

    causal = jnp.tril(jnp.ones((CHUNK, CHUNK), dtype=bool))

    def step(state, inp):
        qc, kc, vc, lfc = inp
        b = jnp.cumsum(lfc, axis=2)
        diff = b[:, :, :, None, :] - b[:, :, None, :, :]
        decay = jnp.exp(jnp.where(causal[None, None, :, :, None], diff, -jnp.inf))
        scores = jnp.einsum('bhtd,bhsd,bhtsd->bhts', qc, kc, decay)
        o = (jnp.einsum('bhts,bhse->bhte', scores, vc)
             + jnp.einsum('bhtd,bhde->bhte', qc * jnp.exp(b), state))
        b_end = b[:, :, -1, :]
        state = (jnp.exp(b_end)[..., None] * state
                 + jnp.einsum('bhsd,bhse->bhde', kc * jnp.exp(b_end[:, :, None, :] - b), vc))
        return state, o

    state0 = jnp.zeros((B, HGRN_HEADS, HGRN_HEAD_DIM, HGRN_HEAD_DIM), jnp.float32)
    _, o = lax.scan(step, state0, (to_chunks(q.astype(jnp.float32)), to_chunks(k),
                                   to_chunks(i_in.astype(jnp.float32)), to_chunks(log_f)))
    o = o.transpose(1, 0, 3, 2, 4).reshape(B, S, HGRN_HEADS, HGRN_HEAD_DIM)
    o = o * lax.rsqrt(jnp.mean(o * o, axis=-1, keepdims=True) + EPS)
    o = o.reshape(B, S, HGRN_WIDTH) * out_gain.astype(jnp.float32) * jax.nn.silu(g.astype(jnp.float32))
    return o.astype(q.dtype)


def chunk_attention(q, k, v, rel_bias, out_gain):
    B, S, _ = q.shape
    n_chunks = S // CHUNK
    scale = ATTN_HEAD_DIM ** -0.5
    qc_all = q.reshape(B, n_chunks, CHUNK, ATTN_HEADS, ATTN_HEAD_DIM).transpose(1, 0, 2, 3, 4)
    pad = ((0, 0), (LEFT_CHUNKS * CHUNK, 0), (0, 0), (0, 0))
    k_pad = jnp.pad(k.reshape(B, S, ATTN_HEADS, ATTN_HEAD_DIM), pad)
    v_pad = jnp.pad(v.reshape(B, S, ATTN_HEADS, ATTN_HEAD_DIM), pad)
    q_off = np.arange(CHUNK)[:, None] + LEFT_CHUNKS * CHUNK
    k_off = np.arange(BAND)[None, :]
    rel_idx = np.clip(q_off - k_off, -MAX_REL, MAX_REL) + MAX_REL
    bias = rel_bias[:, rel_idx].astype(jnp.float32)
    band_pos = jnp.arange(BAND)

    def one_chunk(args):
        qc, c = args
        kb = lax.dynamic_slice_in_dim(k_pad, c * CHUNK, BAND, axis=1)
        vb = lax.dynamic_slice_in_dim(v_pad, c * CHUNK, BAND, axis=1)
        s = jnp.einsum('bqhd,bkhd->bhqk', qc, kb).astype(jnp.float32) * scale + bias
        valid = band_pos >= (LEFT_CHUNKS - c) * CHUNK
        s = jnp.where(valid[None, None, None, :], s, -jnp.inf)
        p = jax.nn.softmax(s, axis=-1).astype(vb.dtype)
        return jnp.einsum('bhqk,bkhd->bqhd', p, vb)

    o = lax.map(one_chunk, (qc_all, jnp.arange(n_chunks)))
    o = o.transpose(1, 0, 2, 3, 4).reshape(B, S, ATTN_WIDTH)
    return rms_norm(o, out_gain)


def memory_cross_attention(h, mem_n, w_q, w_kv, w_o):
    B, S, _ = h.shape
    M = mem_n.shape[1]
    q = (h @ w_q).reshape(B, S, MEM_HEADS, MEM_HEAD_DIM)
    k, v = jnp.split(mem_n @ w_kv, 2, axis=-1)
    k = k.reshape(B, M, MEM_HEADS, MEM_HEAD_DIM)
    v = v.reshape(B, M, MEM_HEADS, MEM_HEAD_DIM)
    s = jnp.einsum('bshd,bmhd->bhsm', q, k).astype(jnp.float32) * (MEM_HEAD_DIM ** -0.5)
    p = jax.nn.softmax(s, axis=-1).astype(v.dtype)
    o = jnp.einsum('bhsm,bmhd->bshd', p, v).reshape(B, S, D_MODEL)
    return o @ w_o


def dense_swiglu(h, w_gate, w_up, w_down):
    return (jax.nn.silu(h @ w_gate) * (h @ w_up)) @ w_down


def moe_swiglu(h, w_router, w_gate, w_up, w_down):
    B, S, D = h.shape
    N = B * S
    hf = h.reshape(N, D)
    logits = (hf @ w_router).astype(jnp.float32)
    top_logits, top_idx = lax.top_k(logits, TOP_K)
    gates = jax.nn.softmax(top_logits, axis=-1)
    NK = N * TOP_K
    expert = top_idx.reshape(NK).astype(jnp.int32)
    token = jnp.repeat(jnp.arange(N, dtype=jnp.int32), TOP_K)
    gate = gates.reshape(NK)
    order = jnp.argsort(expert)
    expert_s, token_s, gate_s = expert[order], token[order], gate[order]
    counts = jnp.bincount(expert, length=N_EXPERTS)
    padded = (counts + MOE_BLOCK - 1) // MOE_BLOCK * MOE_BLOCK
    start = jnp.cumsum(counts) - counts
    padded_end = jnp.cumsum(padded)
    padded_start = padded_end - padded
    dest = padded_start[expert_s] + jnp.arange(NK, dtype=jnp.int32) - start[expert_s]
    n_blocks = -(-(NK + N_EXPERTS * (MOE_BLOCK - 1)) // MOE_BLOCK)
    P = n_blocks * MOE_BLOCK
    row_token = jnp.full((P,), N, jnp.int32).at[dest].set(token_s)
    row_gate = jnp.zeros((P,), jnp.float32).at[dest].set(gate_s)
    block_start = jnp.arange(n_blocks, dtype=jnp.int32) * MOE_BLOCK
    block_expert = jnp.minimum(jnp.searchsorted(padded_end, block_start, side='right'),
                               N_EXPERTS - 1).astype(jnp.int32)
    h_pad = jnp.concatenate([hf, jnp.zeros((1, D), hf.dtype)], axis=0)

    def expert_block(args):
        tok, e = args
        xb = h_pad[tok]
        return (jax.nn.silu(xb @ w_gate[e]) * (xb @ w_up[e])) @ w_down[e]

    yb = lax.map(expert_block, (row_token.reshape(n_blocks, MOE_BLOCK), block_expert))
    y = jnp.zeros((N + 1, D), jnp.float32).at[row_token].add(
        yb.reshape(P, D).astype(jnp.float32) * row_gate[:, None])
    return y[:N].reshape(B, S, D).astype(h.dtype)


def setup_inputs(seed: int = 0) -> dict:
    key = jax.random.key(seed)
    ks = jax.random.split(key, 32)
    f32 = jnp.float32

    def w(k, shape, fan_in):
        return jax.random.normal(k, shape, f32) * (fan_in ** -0.5)

    def gain(k, shape):
        return 1.0 + 0.02 * jax.random.normal(k, shape, f32)

    return {
        'x': jax.random.normal(ks[0], (BATCH, SEQ, D_MODEL), f32),
        'mem': jax.random.normal(ks[1], (BATCH, MEM_LEN, D_MODEL), f32),
        'w_in': w(ks[2], (DEPTH, D_MODEL, IN_WIDTH), D_MODEL),
        'hgrn_lower_bound': 0.1 * jax.random.normal(ks[3], (DEPTH, HGRN_WIDTH), f32),
        'hgrn_out_gain': gain(ks[4], (DEPTH, HGRN_WIDTH)),
        'attn_rel_bias': 0.2 * jax.random.normal(ks[5], (DEPTH, ATTN_HEADS, N_REL), f32),
        'attn_out_gain': gain(ks[6], (DEPTH, ATTN_WIDTH)),
        'w_out': w(ks[7], (DEPTH, MIX_WIDTH, D_MODEL), MIX_WIDTH),
        'norm_mix': gain(ks[8], (DEPTH, D_MODEL)),
        'norm_mem_q': gain(ks[9], (DEPTH, D_MODEL)),
        'norm_mem_kv': gain(ks[10], (DEPTH, D_MODEL)),
        'w_mem_q': w(ks[11], (DEPTH, D_MODEL, D_MODEL), D_MODEL),
        'w_mem_kv': w(ks[12], (DEPTH, D_MODEL, 2 * D_MODEL), D_MODEL),
        'w_mem_o': w(ks[13], (DEPTH, D_MODEL, D_MODEL), D_MODEL),
        'norm_ffn': gain(ks[14], (DEPTH, D_MODEL)),
        'w_ffn_gate': w(ks[15], (N_DENSE, D_MODEL, D_FF), D_MODEL),
        'w_ffn_up': w(ks[16], (N_DENSE, D_MODEL, D_FF), D_MODEL),
        'w_ffn_down': w(ks[17], (N_DENSE, D_FF, D_MODEL), D_FF),
        'w_router': w(ks[18], (N_MOE, D_MODEL, N_EXPERTS), D_MODEL),
        'w_exp_gate': w(ks[19], (N_MOE, N_EXPERTS, D_MODEL, D_FF_EXPERT), D_MODEL),
        'w_exp_up': w(ks[20], (N_MOE, N_EXPERTS, D_MODEL, D_FF_EXPERT), D_MODEL),
        'w_exp_down': w(ks[21], (N_MOE, N_EXPERTS, D_FF_EXPERT, D_MODEL), D_FF_EXPERT),
        'norm_final': gain(ks[22], (D_MODEL,)),
    }


def reference(x, mem, w_in, hgrn_lower_bound, hgrn_out_gain, attn_rel_bias, attn_out_gain,
              w_out, norm_mix, norm_mem_q, norm_mem_kv, w_mem_q, w_mem_kv, w_mem_o, norm_ffn,
              w_ffn_gate, w_ffn_up, w_ffn_down, w_router, w_exp_gate, w_exp_up, w_exp_down,
              norm_final):
    lb_probs = jax.nn.softmax(hgrn_lower_bound.astype(jnp.float32), axis=0)
    lower_bounds = jnp.cumsum(lb_probs, axis=0) - lb_probs[0]
    split_at = list(np.cumsum([HGRN_WIDTH] * 4 + [ATTN_WIDTH] * 2))
    h = x
    for layer in range(DEPTH):
        hn = rms_norm(h, norm_mix[layer])
        proj = hn @ w_in[layer]
        hq, hf_, hi, hg, aq, ak, av = jnp.split(proj, split_at, axis=-1)
        y_hgrn = hgrn2_mixer(hq, hf_, hi, hg, lower_bounds[layer], hgrn_out_gain[layer])
        y_attn = chunk_attention(aq, ak, av, attn_rel_bias[layer], attn_out_gain[layer])
        h = h + jnp.concatenate([y_hgrn, y_attn.astype(y_hgrn.dtype)], axis=-1) @ w_out[layer]
        hn = rms_norm(h, norm_mem_q[layer])
        mn = rms_norm(mem, norm_mem_kv[layer])
        h = h + memory_cross_attention(hn, mn, w_mem_q[layer], w_mem_kv[layer], w_mem_o[layer])
        hn = rms_norm(h, norm_ffn[layer])
        j = layer // 2
        if layer % 2 == 0:
            h = h + dense_swiglu(hn, w_ffn_gate[j], w_ffn_up[j], w_ffn_down[j])
        else:
            h = h + moe_swiglu(hn, w_router[j], w_exp_gate[j], w_exp_up[j], w_exp_down[j])
    return rms_norm(h, norm_final)
```

```python
import functools

import numpy as np
import jax
import jax.numpy as jnp
from jax import lax
from jax.experimental import pallas as pl
from jax.experimental.pallas import tpu as pltpu

F32 = jnp.float32
BF16 = jnp.bfloat16

EPS = 1e-6
CHUNK = 64
LEFT_CHUNKS = 8
MAX_REL = 128
HGRN_HEADS = 4
HGRN_HEAD_DIM = 128
HGRN_WIDTH = HGRN_HEADS * HGRN_HEAD_DIM
ATTN_HEADS = 8
ATTN_HEAD_DIM = 64
ATTN_WIDTH = ATTN_HEADS * ATTN_HEAD_DIM
MEM_HEADS = 4
N_EXPERTS = 8
TOP_K = 2

SUB = 16
ATTN_QBLK = 4 * CHUNK
ATTN_WIN = ATTN_QBLK + LEFT_CHUNKS * CHUNK
MOE_ROWS = 1024
NEG = -1e30

VMEM_LIMIT = 56 * 1024 * 1024


def _params(sem):
    return pltpu.CompilerParams(dimension_semantics=sem, vmem_limit_bytes=VMEM_LIMIT)


def _rms(x, gain):
    ms = jnp.mean(x * x, axis=-1, keepdims=True)
    return x * lax.rsqrt(ms + EPS) * gain


def _softmax_rows(s):
    m = jnp.max(s, axis=-1, keepdims=True)
    e = jnp.exp(s - m)
    return e, jnp.sum(e, axis=-1, keepdims=True)


def _norm_matmul_kernel(x_ref, g_ref, w_ref, o_ref, *, col_chunk):
    xn = _rms(x_ref[...], g_ref[...]).astype(BF16)
    n_out = o_ref.shape[1]
    for c in range(n_out // col_chunk):
        sl = slice(c * col_chunk, (c + 1) * col_chunk)
        o_ref[:, sl] = jnp.dot(xn, w_ref[:, sl], preferred_element_type=F32).astype(o_ref.dtype)


def _norm_matmul(x, gain, w, *, tm, col_chunk, name):
    n, d = x.shape
    n_out = w.shape[1]
    return pl.pallas_call(
        functools.partial(_norm_matmul_kernel, col_chunk=col_chunk),
        grid=(n // tm,),
        in_specs=[
            pl.BlockSpec((tm, d), lambda i: (i, 0)),
            pl.BlockSpec((1, d), lambda i: (0, 0)),
            pl.BlockSpec((d, n_out), lambda i: (0, 0)),
        ],
        out_specs=pl.BlockSpec((tm, n_out), lambda i: (i, 0)),
        out_shape=jax.ShapeDtypeStruct((n, n_out), BF16),
        compiler_params=_params(("parallel",)),
        name=name,
    )(x, gain, w)


def _hgrn_kernel(q_ref, f_ref, i_ref, g_ref, lb_ref, gain_ref, o_ref,
                 st_ref, lf_scr, k_scr, bl_scr, kc_scr, vc_scr, o_scr, *, ts):
    hd = HGRN_HEAD_DIM

    @pl.when(pl.program_id(1) == 0)
    def _():
        st_ref[...] = jnp.zeros_like(st_ref)

    lb = lb_ref[...]
    f = lb + (1.0 - lb) * jax.nn.sigmoid(f_ref[...].astype(F32))
    lf_scr[...] = jnp.log(f)
    k_scr[...] = 1.0 - f

    r = lax.broadcasted_iota(jnp.int32, (SUB, SUB), 0)
    c = lax.broadcasted_iota(jnp.int32, (SUB, SUB), 1)
    ltri = jnp.where(r >= c, 1.0, 0.0).astype(BF16)
    rowid = lax.broadcasted_iota(jnp.int32, (SUB, hd), 0)

    def sub_chunk(j, carry):
        r0 = pl.multiple_of(j * SUB, SUB)
        lf = lf_scr[pl.ds(r0, SUB), :]
        hi = lf.astype(BF16)
        lo = (lf - hi.astype(F32)).astype(BF16)
        bl = (jnp.dot(ltri, hi, preferred_element_type=F32)
              + jnp.dot(ltri, lo, preferred_element_type=F32))
        bl_scr[...] = bl
        kc_scr[...] = k_scr[pl.ds(r0, SUB), :]
        vc_scr[...] = i_ref[pl.ds(r0, SUB), :].astype(F32)
        qf = q_ref[pl.ds(r0, SUB), :].astype(F32)
        for h in range(HGRN_HEADS):
            sl = slice(h * hd, (h + 1) * hd)
            blh = bl[:, sl]
            qh = qf[:, sl]
            st = st_ref[h]
            o = lax.dot_general((qh * jnp.exp(blh)).astype(BF16), st.astype(BF16),
                                (((1,), (1,)), ((), ())), preferred_element_type=F32)
            for s in range(SUB):
                bs = bl_scr[s:s + 1, sl]
                ks = kc_scr[s:s + 1, sl]
                vs = vc_scr[s:s + 1, sl]
                e = jnp.exp(jnp.minimum(blh - bs, 0.0))
                a = jnp.sum(qh * e * ks, axis=-1, keepdims=True)
                a = jnp.where(rowid >= s, a, 0.0)
                o = o + a * vs
            o_scr[pl.ds(r0, SUB), sl] = o
            bend = bl_scr[SUB - 1:SUB, sl]
            kdec = (kc_scr[:, sl] * jnp.exp(bend - blh)).astype(BF16)
            vt = vc_scr[:, sl].T.astype(BF16)
            st_ref[h] = st * jnp.exp(bend) + jnp.dot(vt, kdec, preferred_element_type=F32)
        return carry

    lax.fori_loop(0, ts // SUB, sub_chunk, 0)

    gate = jax.nn.silu(g_ref[...].astype(F32)) * gain_ref[...]
    for h in range(HGRN_HEADS):
        sl = slice(h * hd, (h + 1) * hd)
        o = o_scr[:, sl]
        o = o * lax.rsqrt(jnp.mean(o * o, axis=-1, keepdims=True) + EPS)
        o_ref[:, sl] = (o * gate[:, sl]).astype(o_ref.dtype)


def _hgrn(proj, lb, gain, *, batch, seq, ts):
    n = batch * seq
    w = HGRN_WIDTH
    steps = seq // ts

    def col(j):
        return pl.BlockSpec((ts, w), lambda b, i, j=j: (b * steps + i, j))

    vec = pl.BlockSpec((1, w), lambda b, i: (0, 0))
    return pl.pallas_call(
        functools.partial(_hgrn_kernel, ts=ts),
        grid=(batch, steps),
        in_specs=[col(0), col(1), col(2), col(3), vec, vec],
        out_specs=pl.BlockSpec((ts, w), lambda b, i: (b * steps + i, 0)),
        out_shape=jax.ShapeDtypeStruct((n, w), BF16),
        scratch_shapes=[
            pltpu.VMEM((HGRN_HEADS, HGRN_HEAD_DIM, HGRN_HEAD_DIM), F32),
            pltpu.VMEM((ts, w), F32),
            pltpu.VMEM((ts, w), F32),
            pltpu.VMEM((SUB, w), F32),
            pltpu.VMEM((SUB, w), F32),
            pltpu.VMEM((SUB, w), F32),
            pltpu.VMEM((ts, w), F32),
        ],
        compiler_params=_params(("parallel", "arbitrary")),
        name="hgrn",
    )(proj, proj, proj, proj, lb, gain)


def _attn_block(q, kwin, vwin, bias_ref, lo, gain, o_ref):
    qs = q * jnp.asarray(ATTN_HEAD_DIM ** -0.5, q.dtype)
    lane = lax.broadcasted_iota(jnp.int32, (ATTN_QBLK, 128), 1)
    outs = []
    for p in range(ATTN_HEADS // 2):
        sl = slice(p * 128, (p + 1) * 128)
        qp, kp, vp = qs[:, sl], kwin[:, sl], vwin[:, sl]
        o_pair = None
        for hh in range(2):
            sel = (lane < ATTN_HEAD_DIM) if hh == 0 else (lane >= ATTN_HEAD_DIM)
            qm = jnp.where(sel, qp, jnp.zeros_like(qp))
            s = lax.dot_general(qm, kp, (((1,), (1,)), ((), ())), preferred_element_type=F32)
            s = s + bias_ref[2 * p + hh, :, lo:]
            e, l = _softmax_rows(s)
            o = jnp.dot(e.astype(BF16), vp, preferred_element_type=F32) / l
            o_pair = o if o_pair is None else jnp.where(sel, o, o_pair)
        outs.append(o_pair)
    o = jnp.concatenate(outs, axis=1)
    o_ref[...] = _rms(o, gain).astype(o_ref.dtype)


def _attn_kernel(q_ref, k_ref, v_ref, bias_ref, gain_ref, o_ref):
    i = pl.program_id(1)
    gain = gain_ref[...]
    left = LEFT_CHUNKS * CHUNK

    for blk in range(left // ATTN_QBLK):
        @pl.when(i == blk)
        def _(blk=blk):
            w = (blk + 1) * ATTN_QBLK
            _attn_block(q_ref[...], k_ref[0:w, :], v_ref[0:w, :], bias_ref, ATTN_WIN - w, gain, o_ref)

    @pl.when(i >= left // ATTN_QBLK)
    def _():
        start = pl.multiple_of(i * ATTN_QBLK - left, ATTN_QBLK)
        _attn_block(q_ref[...], k_ref[pl.ds(start, ATTN_WIN), :], v_ref[pl.ds(start, ATTN_WIN), :],
                    bias_ref, 0, gain, o_ref)


def _attn_bias(rel_bias):
    qi = np.arange(ATTN_QBLK)[:, None]
    kj = np.arange(ATTN_WIN)[None, :]
    rel = np.clip(LEFT_CHUNKS * CHUNK + qi - kj, -MAX_REL, MAX_REL) + MAX_REL
    dchunk = qi // CHUNK + LEFT_CHUNKS - kj // CHUNK
    in_band = (dchunk >= 0) & (dchunk <= LEFT_CHUNKS)
    bias = rel_bias.astype(F32)[:, rel]
    return jnp.where(in_band[None], bias, NEG)


def _attn(proj, bias, gain, *, batch, seq):
    n = batch * seq
    w = ATTN_WIDTH
    steps = seq // ATTN_QBLK
    qcol = 4 * HGRN_WIDTH // w
    return pl.pallas_call(
        _attn_kernel,
        grid=(batch, steps),
        in_specs=[
            pl.BlockSpec((ATTN_QBLK, w), lambda b, i: (b * steps + i, qcol)),
            pl.BlockSpec((seq, w), lambda b, i: (b, qcol + 1)),
            pl.BlockSpec((seq, w), lambda b, i: (b, qcol + 2)),
            pl.BlockSpec((ATTN_HEADS, ATTN_QBLK, ATTN_WIN), lambda b, i: (0, 0, 0)),
            pl.BlockSpec((1, w), lambda b, i: (0, 0)),
        ],
        out_specs=pl.BlockSpec((ATTN_QBLK, w), lambda b, i: (b * steps + i, 0)),
        out_shape=jax.ShapeDtypeStruct((n, w), BF16),
        compiler_params=_params(("parallel", "arbitrary")),
        name="attn",
    )(proj, proj, proj, bias, gain)


def _post_kernel(h_ref, yh_ref, ya_ref, wout_ref, gq_ref, wq_ref, kv_ref, wo_ref, o_ref):
    d = h_ref.shape[1]
    hw = yh_ref.shape[1]
    h1 = (h_ref[...]
          + jnp.dot(yh_ref[...], wout_ref[0:hw, :], preferred_element_type=F32)
          + jnp.dot(ya_ref[...], wout_ref[hw:, :], preferred_element_type=F32))
    hn = _rms(h1, gq_ref[...]).astype(BF16)
    dh = d // MEM_HEADS
    q = (jnp.dot(hn, wq_ref[...], preferred_element_type=F32) * (dh ** -0.5)).astype(BF16)
    outs = []
    for hd in range(MEM_HEADS):
        sl = slice(hd * dh, (hd + 1) * dh)
        k = kv_ref[:, sl]
        v = kv_ref[:, d + hd * dh:d + (hd + 1) * dh]
        s = lax.dot_general(q[:, sl], k, (((1,), (1,)), ((), ())), preferred_element_type=F32)
        e, l = _softmax_rows(s)
        outs.append((jnp.dot(e.astype(BF16), v, preferred_element_type=F32) / l).astype(BF16))
    o = jnp.concatenate(outs, axis=1)
    o_ref[...] = h1 + jnp.dot(o, wo_ref[...], preferred_element_type=F32)


def _post(h, yh, ya, w_out, gq, wq, kv, wo, *, seq, mem_len, tm):
    n, d = h.shape
    per_batch = seq // tm
    const = lambda i: (0, 0)
    return pl.pallas_call(
        _post_kernel,
        grid=(n // tm,),
        in_specs=[
            pl.BlockSpec((tm, d), lambda i: (i, 0)),
            pl.BlockSpec((tm, yh.shape[1]), lambda i: (i, 0)),
            pl.BlockSpec((tm, ya.shape[1]), lambda i: (i, 0)),
            pl.BlockSpec(w_out.shape, const),
            pl.BlockSpec((1, d), const),
            pl.BlockSpec(wq.shape, const),
            pl.BlockSpec((mem_len, 2 * d), lambda i: (i // per_batch, 0)),
            pl.BlockSpec(wo.shape, const),
        ],
        out_specs=pl.BlockSpec((tm, d), lambda i: (i, 0)),
        out_shape=jax.ShapeDtypeStruct((n, d), F32),
        compiler_params=_params(("parallel",)),
        name="post",
    )(h, yh, ya, w_out, gq, wq, kv, wo)


def _ffn_kernel(h_ref, g_ref, wg_ref, wu_ref, wd_ref, o_ref, xn_scr, acc_scr):
    f = pl.program_id(1)

    @pl.when(f == 0)
    def _():
        xn_scr[...] = _rms(h_ref[...], g_ref[...]).astype(BF16)

    xn = xn_scr[...]
    a = jnp.dot(xn, wg_ref[...], preferred_element_type=F32)
    u = jnp.dot(xn, wu_ref[...], preferred_element_type=F32)
    act = (jax.nn.silu(a) * u).astype(BF16)
    part = jnp.dot(act, wd_ref[...], preferred_element_type=F32)

    @pl.when(f == 0)
    def _():
        acc_scr[...] = part

    @pl.when(f > 0)
    def _():
        acc_scr[...] += part

    @pl.when(f == pl.num_programs(1) - 1)
    def _():
        o_ref[...] = h_ref[...] + acc_scr[...]


def _ffn(h, gain, wg, wu, wd, *, tm, tf):
    n, d = h.shape
    dff = wg.shape[1]
    return pl.pallas_call(
        _ffn_kernel,
        grid=(n // tm, dff // tf),
        in_specs=[
            pl.BlockSpec((tm, d), lambda i, f: (i, 0)),
            pl.BlockSpec((1, d), lambda i, f: (0, 0)),
            pl.BlockSpec((d, tf), lambda i, f: (0, f)),
            pl.BlockSpec((d, tf), lambda i, f: (0, f)),
            pl.BlockSpec((tf, d), lambda i, f: (f, 0)),
        ],
        out_specs=pl.BlockSpec((tm, d), lambda i, f: (i, 0)),
        out_shape=jax.ShapeDtypeStruct((n, d), F32),
        scratch_shapes=[pltpu.VMEM((tm, d), BF16), pltpu.VMEM((tm, d), F32)],
        compiler_params=_params(("parallel", "arbitrary")),
        name="ffn",
    )(h, gain, wg, wu, wd)


def _router_kernel(h_ref, g_ref, w1_ref, w2_ref, tri_ref, ri_ref, rg_ref, cnt_ref, carry_scr):
    i = pl.program_id(0)
    tm = h_ref.shape[0]

    @pl.when(i == 0)
    def _():
        carry_scr[...] = jnp.zeros_like(carry_scr)

    hn = _rms(h_ref[...], g_ref[...])
    a1 = hn.astype(BF16)
    a2 = (hn - a1.astype(F32)).astype(BF16)
    lg = (jnp.dot(a1, w1_ref[...], preferred_element_type=F32)
          + jnp.dot(a1, w2_ref[...], preferred_element_type=F32)
          + jnp.dot(a2, w1_ref[...], preferred_element_type=F32))
    lgt = lg.T[0:N_EXPERTS, :]
    ids = lax.broadcasted_iota(jnp.int32, (N_EXPERTS, tm), 0)
    m1 = jnp.max(lgt, axis=0, keepdims=True)
    i1 = jnp.min(jnp.where(lgt == m1, ids, N_EXPERTS), axis=0, keepdims=True)
    rest = jnp.where(ids == i1, -jnp.inf, lgt)
    m2 = jnp.max(rest, axis=0, keepdims=True)
    i2 = jnp.min(jnp.where(rest == m2, ids, N_EXPERTS), axis=0, keepdims=True)
    e = jnp.exp(m2 - m1)
    g1 = 1.0 / (1.0 + e)
    g2 = e * g1
    hit1 = ids == i1
    hit2 = ids == i2
    onehot = jnp.where(hit1 | hit2, 1.0, 0.0)
    before = jnp.dot(onehot.astype(BF16), tri_ref[...], preferred_element_type=F32) + carry_scr[...][:, 0:1]
    r1 = jnp.sum(jnp.where(hit1, before, 0.0), axis=0, keepdims=True)
    r2 = jnp.sum(jnp.where(hit2, before, 0.0), axis=0, keepdims=True)
    carry_scr[...] = carry_scr[...] + jnp.sum(onehot, axis=1, keepdims=True)
    cnt_ref[...] = carry_scr[...]
    row = lax.broadcasted_iota(jnp.int32, (8, tm), 0)
    ri_ref[...] = jnp.where(row == 0, i1, jnp.where(row == 1, i2, jnp.where(
        row == 2, r1.astype(jnp.int32), jnp.where(row == 3, r2.astype(jnp.int32), 0))))
    rg_ref[...] = jnp.where(row == 0, g1, jnp.where(row == 1, g2, 0.0))


def _router(h, gain, w1, w2, tri, *, tm):
    n, d = h.shape
    return pl.pallas_call(
        _router_kernel,
        grid=(n // tm,),
        in_specs=[
            pl.BlockSpec((tm, d), lambda i: (i, 0)),
            pl.BlockSpec((1, d), lambda i: (0, 0)),
            pl.BlockSpec(w1.shape, lambda i: (0, 0)),
            pl.BlockSpec(w2.shape, lambda i: (0, 0)),
            pl.BlockSpec((tm, tm), lambda i: (0, 0)),
        ],
        out_specs=[
            pl.BlockSpec((8, tm), lambda i: (0, i)),
            pl.BlockSpec((8, tm), lambda i: (0, i)),
            pl.BlockSpec((N_EXPERTS, 128), lambda i: (0, 0)),
        ],
        out_shape=[
            jax.ShapeDtypeStruct((8, n), jnp.int32),
            jax.ShapeDtypeStruct((8, n), F32),
            jax.ShapeDtypeStruct((N_EXPERTS, 128), F32),
        ],
        scratch_shapes=[pltpu.VMEM((N_EXPERTS, 128), F32)],
        compiler_params=_params(("arbitrary",)),
        name="router",
    )(h, gain, w1, w2, tri)


def _row_copy(src, dst, sem):
    return pltpu.make_async_copy(src, dst, sem)


def _dispatch_kernel(dest_ref, h_ref, g_ref, xs_in_ref, xs_ref, idx_smem, buf, sem, isem):
    del xs_in_ref
    i = pl.program_id(0)
    tm = h_ref.shape[0]
    icopy = _row_copy(dest_ref.at[i], idx_smem, isem)
    icopy.start()
    buf[...] = _rms(h_ref[...], g_ref[...])
    icopy.wait()

    def issue(t, carry):
        for k in range(TOP_K):
            _row_copy(buf.at[pl.ds(t, 1)], xs_ref.at[pl.ds(idx_smem[k, t], 1)], sem).start()
        return carry

    lax.fori_loop(0, tm, issue, 0)

    def drain(t, carry):
        for k in range(TOP_K):
            _row_copy(buf.at[pl.ds(t, 1)], xs_ref.at[pl.ds(idx_smem[k, t], 1)], sem).wait()
        return carry

    lax.fori_loop(0, tm, drain, 0)


def _dispatch(dest, h, gain, xs0, *, tm):
    n, d = h.shape
    return pl.pallas_call(
        _dispatch_kernel,
        grid=(n // tm,),
        in_specs=[
            pl.BlockSpec(memory_space=pl.ANY),
            pl.BlockSpec((tm, d), lambda i: (i, 0)),
            pl.BlockSpec((1, d), lambda i: (0, 0)),
            pl.BlockSpec(memory_space=pl.ANY),
        ],
        out_specs=pl.BlockSpec(memory_space=pl.ANY),
        out_shape=jax.ShapeDtypeStruct(xs0.shape, xs0.dtype),
        scratch_shapes=[
            pltpu.SMEM((TOP_K, tm), jnp.int32),
            pltpu.VMEM((tm, d), F32),
            pltpu.SemaphoreType.DMA,
            pltpu.SemaphoreType.DMA,
        ],
        input_output_aliases={3: 0},
        compiler_params=_params(("arbitrary",)),
        name="dispatch",
    )(dest, h, gain, xs0)


def _experts_kernel(be_ref, nu_ref, x_ref, wg_ref, wu_ref, wd_ref, o_ref, xb_scr, acc_scr):
    j = pl.program_id(0)
    f = pl.program_id(1)

    @pl.when(j < nu_ref[0])
    def _():
        @pl.when(f == 0)
        def _():
            xb_scr[...] = x_ref[...].astype(BF16)

        xb = xb_scr[...]
        a = jnp.dot(xb, wg_ref[...], preferred_element_type=F32)
        u = jnp.dot(xb, wu_ref[...], preferred_element_type=F32)
        act = (jax.nn.silu(a) * u).astype(BF16)
        part = jnp.dot(act, wd_ref[...], preferred_element_type=F32)

        @pl.when(f == 0)
        def _():
            acc_scr[...] = part

        @pl.when(f > 0)
        def _():
            acc_scr[...] += part

        @pl.when(f == pl.num_programs(1) - 1)
        def _():
            o_ref[...] = acc_scr[...]

    @pl.when((j >= nu_ref[0]) & (f == pl.num_programs(1) - 1))
    def _():
        o_ref[...] = jnp.zeros_like(o_ref)


def _experts(block_expert, n_used, xs, wg, wu, wd, *, tf):
    p, d = xs.shape
    dff = wg.shape[2]
    nf = dff // tf
    n_blocks = p // MOE_ROWS

    def row(j, f, be, nu):
        return (jnp.minimum(j, nu[0] - 1), 0)

    def fcol(j, f, nu):
        return jnp.where(j < nu[0], f, nf - 1)

    grid_spec = pltpu.PrefetchScalarGridSpec(
        num_scalar_prefetch=2,
        grid=(n_blocks, nf),
        in_specs=[
            pl.BlockSpec((MOE_ROWS, d), row),
            pl.BlockSpec((None, d, tf), lambda j, f, be, nu: (be[j], 0, fcol(j, f, nu))),
            pl.BlockSpec((None, d, tf), lambda j, f, be, nu: (be[j], 0, fcol(j, f, nu))),
            pl.BlockSpec((None, tf, d), lambda j, f, be, nu: (be[j], fcol(j, f, nu), 0)),
        ],
        out_specs=pl.BlockSpec((MOE_ROWS, d), lambda j, f, be, nu: (j, 0)),
        scratch_shapes=[pltpu.VMEM((MOE_ROWS, d), BF16), pltpu.VMEM((MOE_ROWS, d), F32)],
    )
    return pl.pallas_call(
        _experts_kernel,
        grid_spec=grid_spec,
        out_shape=jax.ShapeDtypeStruct((p, d), F32),
        compiler_params=_params(("arbitrary", "arbitrary")),
        name="experts",
    )(block_expert, n_used, xs, wg, wu, wd)


def _combine_kernel(dest_ref, h_ref, rg_ref, gf_ref, yb_ref, o_ref, idx_smem, r_buf, sem, isem,
                    *, final_norm):
    i = pl.program_id(0)
    tm = h_ref.shape[0]
    icopy = _row_copy(dest_ref.at[i], idx_smem, isem)
    icopy.start()
    icopy.wait()

    def issue(t, carry):
        for k in range(TOP_K):
            _row_copy(yb_ref.at[pl.ds(idx_smem[k, t], 1)], r_buf.at[k, pl.ds(t, 1)], sem).start()
        return carry

    lax.fori_loop(0, tm, issue, 0)
    gates = rg_ref[...].T

    def drain(t, carry):
        for k in range(TOP_K):
            _row_copy(yb_ref.at[pl.ds(idx_smem[k, t], 1)], r_buf.at[k, pl.ds(t, 1)], sem).wait()
        return carry

    lax.fori_loop(0, tm, drain, 0)
    out = h_ref[...] + gates[:, 0:1] * r_buf[0] + gates[:, 1:2] * r_buf[1]
    if final_norm:
        out = _rms(out, gf_ref[...])
    o_ref[...] = out


def _combine(dest, h, rg, gain_final, yb, *, tm, final_norm):
    n, d = h.shape
    return pl.pallas_call(
        functools.partial(_combine_kernel, final_norm=final_norm),
        grid=(n // tm,),
        in_specs=[
            pl.BlockSpec(memory_space=pl.ANY),
            pl.BlockSpec((tm, d), lambda i: (i, 0)),
            pl.BlockSpec((8, tm), lambda i: (0, i)),
            pl.BlockSpec((1, d), lambda i: (0, 0)),
            pl.BlockSpec(memory_space=pl.ANY),
        ],
        out_specs=pl.BlockSpec((tm, d), lambda i: (i, 0)),
        out_shape=jax.ShapeDtypeStruct((n, d), F32),
        scratch_shapes=[
            pltpu.SMEM((TOP_K, tm), jnp.int32),
            pltpu.VMEM((TOP_K, tm, d), F32),
            pltpu.SemaphoreType.DMA,
            pltpu.SemaphoreType.DMA,
        ],
        compiler_params=_params(("arbitrary",)),
        name="combine",
    )(dest, h, rg, gain_final, yb)


def _moe(h, gain, w_router, wg, wu, wd, gain_final, *, final_norm, tm_route, tm_rows):
    n, d = h.shape
    wr = jnp.zeros((d, 128), F32).at[:, :N_EXPERTS].set(w_router.astype(F32))
    w1 = wr.astype(BF16)
    w2 = (wr - w1.astype(F32)).astype(BF16)
    tri = jnp.asarray(np.triu(np.ones((tm_route, tm_route), np.float32), 1), BF16)
    ri, rg, cnt = _router(h, gain, w1, w2, tri, tm=tm_route)

    counts = cnt[:, 0].astype(jnp.int32)
    padded = (counts + MOE_ROWS - 1) // MOE_ROWS * MOE_ROWS
    padded_end = jnp.cumsum(padded)
    padded_start = padded_end - padded
    n_blocks = -(-(n * TOP_K + N_EXPERTS * (MOE_ROWS - 1)) // MOE_ROWS)
    block_start = jnp.arange(n_blocks, dtype=jnp.int32) * MOE_ROWS
    block_expert = jnp.minimum(
        jnp.sum(block_start[:, None] >= padded_end[None, :], axis=1), N_EXPERTS - 1).astype(jnp.int32)
    n_used = (padded_end[-1:] // MOE_ROWS).astype(jnp.int32)
    onehot = ri[0:TOP_K, :, None] == jnp.arange(N_EXPERTS, dtype=jnp.int32)
    dest = jnp.sum(jnp.where(onehot, padded_start, 0), axis=-1) + ri[TOP_K:2 * TOP_K]
    dest = dest.astype(jnp.int32).reshape(TOP_K, n // tm_rows, tm_rows).transpose(1, 0, 2)

    xs0 = jnp.zeros((n_blocks * MOE_ROWS, d), F32)
    xs = _dispatch(dest, h, gain, xs0, tm=tm_rows)
    yb = _experts(block_expert, n_used, xs, wg, wu, wd, tf=512)
    return _combine(dest, h, rg, gain_final, yb, tm=tm_rows, final_norm=final_norm)


def kernel(x, mem, w_in, hgrn_lower_bound, hgrn_out_gain, attn_rel_bias, attn_out_gain, w_out,
           norm_mix, norm_mem_q, norm_mem_kv, w_mem_q, w_mem_kv, w_mem_o, norm_ffn,
           w_ffn_gate, w_ffn_up, w_ffn_down, w_router, w_exp_gate, w_exp_up, w_exp_down, norm_final):
    batch, seq, d = x.shape
    mem_len = mem.shape[1]
    depth = w_in.shape[0]
    n = batch * seq
    assert seq % 512 == 0 and seq >= ATTN_WIN and n % 1024 == 0

    lb_probs = jax.nn.softmax(hgrn_lower_bound.astype(F32), axis=0)
    lower_bounds = jnp.cumsum(lb_probs, axis=0) - lb_probs[0]

    h = x.reshape(n, d)
    mem2 = mem.reshape(batch * mem_len, d)
    row = lambda v: v.reshape(1, -1).astype(F32)
    for layer in range(depth):
        proj = _norm_matmul(h, row(norm_mix[layer]), w_in[layer].astype(BF16),
                            tm=512, col_chunk=512, name="in_proj")
        y_hgrn = _hgrn(proj, row(lower_bounds[layer]), row(hgrn_out_gain[layer]),
                       batch=batch, seq=seq, ts=512)
        y_attn = _attn(proj, _attn_bias(attn_rel_bias[layer]), row(attn_out_gain[layer]),
                       batch=batch, seq=seq)
        kv = _norm_matmul(mem2, row(norm_mem_kv[layer]), w_mem_kv[layer].astype(BF16),
                          tm=min(512, batch * mem_len), col_chunk=512, name="mem_kv")
        h = _post(h, y_hgrn, y_attn, w_out[layer].astype(BF16), row(norm_mem_q[layer]),
                  w_mem_q[layer].astype(BF16), kv, w_mem_o[layer].astype(BF16),
                  seq=seq, mem_len=mem_len, tm=512)
        j = layer // 2
        if layer % 2 == 0:
            h = _ffn(h, row(norm_ffn[layer]), w_ffn_gate[j].astype(BF16), w_ffn_up[j].astype(BF16),
                     w_ffn_down[j].astype(BF16), tm=1024, tf=256)
        else:
            last = layer == depth - 1
            h = _moe(h, row(norm_ffn[layer]), w_router[j], w_exp_gate[j].astype(BF16),
                     w_exp_up[j].astype(BF16), w_exp_down[j].astype(BF16), row(norm_final),
                     final_norm=last, tm_route=512, tm_rows=256)
    if depth % 2 == 1:
        h = _norm_only(h, row(norm_final))
    return h.reshape(batch, seq, d)


def _norm_only_kernel(x_ref, g_ref, o_ref):
    o_ref[...] = _rms(x_ref[...], g_ref[...])


def _norm_only(x, gain, tm=1024):
    n, d = x.shape
    return pl.pallas_call(
        _norm_only_kernel,
        grid=(n // tm,),
        in_specs=[pl.BlockSpec((tm, d), lambda i: (i, 0)), pl.BlockSpec((1, d), lambda i: (0, 0))],
        out_specs=pl.BlockSpec((tm, d), lambda i: (i, 0)),
        out_shape=jax.ShapeDtypeStruct((n, d), F32),
        compiler_params=_params(("parallel",)),
        name="final_norm",
    )(x, gain)
```

```python
import functools

import numpy as np
import jax
import jax.numpy as jnp
from jax import lax
from jax.experimental import pallas as pl
from jax.experimental.pallas import tpu as pltpu

F32 = jnp.float32
BF16 = jnp.bfloat16

EPS = 1e-6
CHUNK = 64
LEFT_CHUNKS = 8
MAX_REL = 128
HGRN_HEADS = 4
HGRN_HEAD_DIM = 128
HGRN_WIDTH = HGRN_HEADS * HGRN_HEAD_DIM
ATTN_HEADS = 8
ATTN_HEAD_DIM = 64
ATTN_WIDTH = ATTN_HEADS * ATTN_HEAD_DIM
MEM_HEADS = 4
N_EXPERTS = 8
TOP_K = 2

SUB = 16
FAST = 32
FAST_MAX_DECAY = 60.0
ATTN_QBLK = 4 * CHUNK
ATTN_WIN = ATTN_QBLK + LEFT_CHUNKS * CHUNK
MOE_ROWS = 1024
NEG = -1e30

VMEM_LIMIT = 56 * 1024 * 1024


def _params(sem):
    return pltpu.CompilerParams(dimension_semantics=sem, vmem_limit_bytes=VMEM_LIMIT)


def _rms(x, gain):
    ms = jnp.mean(x * x, axis=-1, keepdims=True)
    return x * lax.rsqrt(ms + EPS) * gain


def _softmax_rows(s):
    m = jnp.max(s, axis=-1, keepdims=True)
    e = jnp.exp(s - m)
    return e, jnp.sum(e, axis=-1, keepdims=True)


def _norm_matmul_kernel(x_ref, g_ref, w_ref, o_ref, *, col_chunk):
    xn = _rms(x_ref[...], g_ref[...]).astype(BF16)
    n_out = o_ref.shape[1]
    for c in range(n_out // col_chunk):
        sl = slice(c * col_chunk, (c + 1) * col_chunk)
        o_ref[:, sl] = jnp.dot(xn, w_ref[:, sl], preferred_element_type=F32).astype(o_ref.dtype)


def _norm_matmul(x, gain, w, *, tm, col_chunk, name):
    n, d = x.shape
    n_out = w.shape[1]
    return pl.pallas_call(
        functools.partial(_norm_matmul_kernel, col_chunk=col_chunk),
        grid=(n // tm,),
        in_specs=[
            pl.BlockSpec((tm, d), lambda i: (i, 0)),
            pl.BlockSpec((1, d), lambda i: (0, 0)),
            pl.BlockSpec((d, n_out), lambda i: (0, 0)),
        ],
        out_specs=pl.BlockSpec((tm, n_out), lambda i: (i, 0)),
        out_shape=jax.ShapeDtypeStruct((n, n_out), BF16),
        compiler_params=_params(("parallel",)),
        name=name,
    )(x, gain, w)


def _hgrn_exact_tile(q_ref, i_ref, st_ref, lf_scr, k_scr, bl_scr, kc_scr, vc_scr, o_scr, ts):
    hd = HGRN_HEAD_DIM
    r = lax.broadcasted_iota(jnp.int32, (SUB, SUB), 0)
    c = lax.broadcasted_iota(jnp.int32, (SUB, SUB), 1)
    ltri = jnp.where(r >= c, 1.0, 0.0).astype(BF16)
    rowid = lax.broadcasted_iota(jnp.int32, (SUB, hd), 0)

    def sub_chunk(j, carry):
        r0 = pl.multiple_of(j * SUB, SUB)
        lf = lf_scr[pl.ds(r0, SUB), :]
        hi = lf.astype(BF16)
        lo = (lf - hi.astype(F32)).astype(BF16)
        bl = (jnp.dot(ltri, hi, preferred_element_type=F32)
              + jnp.dot(ltri, lo, preferred_element_type=F32))
        bl_scr[...] = bl
        kc_scr[...] = k_scr[pl.ds(r0, SUB), :]
        vc_scr[...] = i_ref[pl.ds(r0, SUB), :].astype(F32)
        qf = q_ref[pl.ds(r0, SUB), :].astype(F32)
        for h in range(HGRN_HEADS):
            sl = slice(h * hd, (h + 1) * hd)
            blh = bl[:, sl]
            qh = qf[:, sl]
            st = st_ref[h]
            o = lax.dot_general((qh * jnp.exp(blh)).astype(BF16), st.astype(BF16),
                                (((1,), (1,)), ((), ())), preferred_element_type=F32)
            for s in range(SUB):
                bs = bl_scr[s:s + 1, sl]
                ks = kc_scr[s:s + 1, sl]
                vs = vc_scr[s:s + 1, sl]
                e = jnp.exp(jnp.minimum(blh - bs, 0.0))
                a = jnp.sum(qh * e * ks, axis=-1, keepdims=True)
                a = jnp.where(rowid >= s, a, 0.0)
                o = o + a * vs
            o_scr[pl.ds(r0, SUB), sl] = o
            bend = bl_scr[SUB - 1:SUB, sl]
            kdec = (kc_scr[:, sl] * jnp.exp(bend - blh)).astype(BF16)
            vt = vc_scr[:, sl].T.astype(BF16)
            st_ref[h] = st * jnp.exp(bend) + jnp.dot(vt, kdec, preferred_element_type=F32)
        return carry

    lax.fori_loop(0, ts // SUB, sub_chunk, 0)


def _hgrn_fast_tile(q_ref, i_ref, st_ref, k_scr, b_scr, o_scr, ts):
    hd = HGRN_HEAD_DIM
    r = lax.broadcasted_iota(jnp.int32, (CHUNK, CHUNK), 0)
    c = lax.broadcasted_iota(jnp.int32, (CHUNK, CHUNK), 1)
    causal = r >= c
    first_half = lax.broadcasted_iota(jnp.int32, (CHUNK, HGRN_WIDTH), 0) < FAST

    def block(j, carry):
        r0 = pl.multiple_of(j * CHUNK, CHUNK)
        b = b_scr[pl.ds(r0, CHUNK), :]
        b_mid = b[FAST - 1:FAST, :]
        rel = b - jnp.where(first_half, b_mid, 0.0)
        e_mid = jnp.exp(b_mid)
        e_tail = jnp.exp(b[CHUNK - 1:CHUNK, :])
        kdf = k_scr[pl.ds(r0, CHUNK), :] * jnp.exp(-rel)
        qd = (q_ref[pl.ds(r0, CHUNK), :].astype(F32) * jnp.exp(rel)).astype(BF16)
        kd = kdf.astype(BF16)
        v = i_ref[pl.ds(r0, CHUNK), :]
        kdec = (kdf * e_tail).astype(BF16)
        vt = v.astype(F32).T.astype(BF16)
        for h in range(HGRN_HEADS):
            sl = slice(h * hd, (h + 1) * hd)
            st = st_ref[h]
            a = lax.dot_general(qd[:, sl], kd[:, sl], (((1,), (1,)), ((), ())),
                                preferred_element_type=F32)
            a = jnp.where(causal, a, 0.0).astype(BF16)
            st_mid = (st * e_mid[:, sl]).astype(BF16)
            o_scr[pl.ds(r0, CHUNK), sl] = (
                jnp.dot(a, v[:, sl], preferred_element_type=F32)
                + lax.dot_general(qd[:, sl], st_mid, (((1,), (1,)), ((), ())),
                                  preferred_element_type=F32))
            st_ref[h] = (st * (e_mid[:, sl] * e_tail[:, sl])
                         + jnp.dot(vt[sl, :], kdec[:, sl], preferred_element_type=F32))
        return carry

    lax.fori_loop(0, ts // CHUNK, block, 0, unroll=True)


def _hgrn_kernel(q_ref, f_ref, i_ref, g_ref, lb_ref, gain_ref, tri_ref, o_ref,
                 st_ref, lf_scr, k_scr, b_scr, bl_scr, kc_scr, vc_scr, o_scr, *, ts):
    hd = HGRN_HEAD_DIM

    @pl.when(pl.program_id(1) == 0)
    def _():
        st_ref[...] = jnp.zeros_like(st_ref)

    lb = lb_ref[...]
    f = lb + (1.0 - lb) * jax.nn.sigmoid(f_ref[...].astype(F32))
    lf = jnp.log(f)
    lf_scr[...] = lf
    k_scr[...] = 1.0 - f
    hi = lf.astype(BF16)
    lo = (lf - hi.astype(F32)).astype(BF16)
    b = (jnp.dot(tri_ref[...], hi, preferred_element_type=F32)
         + jnp.dot(tri_ref[...], lo, preferred_element_type=F32))
    b_scr[...] = b
    mild = jnp.min(b) >= -FAST_MAX_DECAY

    @pl.when(mild)
    def _():
        _hgrn_fast_tile(q_ref, i_ref, st_ref, k_scr, b_scr, o_scr, ts)

    @pl.when(jnp.logical_not(mild))
    def _():
        _hgrn_exact_tile(q_ref, i_ref, st_ref, lf_scr, k_scr, bl_scr, kc_scr, vc_scr, o_scr, ts)

    gate = jax.nn.silu(g_ref[...].astype(F32)) * gain_ref[...]
    for h in range(HGRN_HEADS):
        sl = slice(h * hd, (h + 1) * hd)
        o = o_scr[:, sl]
        o = o * lax.rsqrt(jnp.mean(o * o, axis=-1, keepdims=True) + EPS)
        o_ref[:, sl] = (o * gate[:, sl]).astype(o_ref.dtype)


def _hgrn(proj, lb, gain, *, batch, seq, ts):
    n = batch * seq
    w = HGRN_WIDTH
    steps = seq // ts
    rows = np.arange(ts)
    tri = jnp.asarray((rows[:, None] // FAST == rows[None, :] // FAST) & (rows[:, None] >= rows[None, :]), BF16)

    def col(j):
        return pl.BlockSpec((ts, w), lambda b, i, j=j: (b * steps + i, j))

    vec = pl.BlockSpec((1, w), lambda b, i: (0, 0))
    return pl.pallas_call(
        functools.partial(_hgrn_kernel, ts=ts),
        grid=(batch, steps),
        in_specs=[col(0), col(1), col(2), col(3), vec, vec, pl.BlockSpec((ts, ts), lambda b, i: (0, 0))],
        out_specs=pl.BlockSpec((ts, w), lambda b, i: (b * steps + i, 0)),
        out_shape=jax.ShapeDtypeStruct((n, w), BF16),
        scratch_shapes=[
            pltpu.VMEM((HGRN_HEADS, HGRN_HEAD_DIM, HGRN_HEAD_DIM), F32),
            pltpu.VMEM((ts, w), F32),
            pltpu.VMEM((ts, w), F32),
            pltpu.VMEM((ts, w), F32),
            pltpu.VMEM((SUB, w), F32),
            pltpu.VMEM((SUB, w), F32),
            pltpu.VMEM((SUB, w), F32),
            pltpu.VMEM((ts, w), F32),
        ],
        compiler_params=_params(("parallel", "arbitrary")),
        name="hgrn",
    )(proj, proj, proj, proj, lb, gain, tri)


def _attn_block(q, kwin, vwin, bias_ref, lo, gain, o_ref):
    qs = q * jnp.asarray(ATTN_HEAD_DIM ** -0.5, q.dtype)
    lane = lax.broadcasted_iota(jnp.int32, (ATTN_QBLK, 128), 1)
    outs = []
    for p in range(ATTN_HEADS // 2):
        sl = slice(p * 128, (p + 1) * 128)
        qp, kp, vp = qs[:, sl], kwin[:, sl], vwin[:, sl]
        o_pair = None
        for hh in range(2):
            sel = (lane < ATTN_HEAD_DIM) if hh == 0 else (lane >= ATTN_HEAD_DIM)
            qm = jnp.where(sel, qp, jnp.zeros_like(qp))
            s = lax.dot_general(qm, kp, (((1,), (1,)), ((), ())), preferred_element_type=F32)
            s = s + bias_ref[2 * p + hh, :, lo:]
            e, l = _softmax_rows(s)
            o = jnp.dot(e.astype(BF16), vp, preferred_element_type=F32) / l
            o_pair = o if o_pair is None else jnp.where(sel, o, o_pair)
        outs.append(o_pair)
    o = jnp.concatenate(outs, axis=1)
    o_ref[...] = _rms(o, gain).astype(o_ref.dtype)


def _attn_kernel(q_ref, k_ref, v_ref, bias_ref, gain_ref, o_ref):
    i = pl.program_id(1)
    gain = gain_ref[...]
    left = LEFT_CHUNKS * CHUNK

    for blk in range(left // ATTN_QBLK):
        @pl.when(i == blk)
        def _(blk=blk):
            w = (blk + 1) * ATTN_QBLK
            _attn_block(q_ref[...], k_ref[0:w, :], v_ref[0:w, :], bias_ref, ATTN_WIN - w, gain, o_ref)

    @pl.when(i >= left // ATTN_QBLK)
    def _():
        start = pl.multiple_of(i * ATTN_QBLK - left, ATTN_QBLK)
        _attn_block(q_ref[...], k_ref[pl.ds(start, ATTN_WIN), :], v_ref[pl.ds(start, ATTN_WIN), :],
                    bias_ref, 0, gain, o_ref)


def _attn_bias(rel_bias):
    q, w = ATTN_QBLK, ATTN_WIN
    span = q + w - 1
    r = LEFT_CHUNKS * CHUNK + (q - 1) - np.arange(span)
    vec = rel_bias.astype(F32)[:, np.clip(r, -MAX_REL, MAX_REL) + MAX_REL]
    skew = jnp.tile(vec, (1, q + 1))[:, :q * (span + 1)].reshape(-1, q, span + 1)
    bias = skew[:, ::-1, :w]
    qi = np.arange(q)[:, None]
    kj = np.arange(w)[None, :]
    dchunk = qi // CHUNK + LEFT_CHUNKS - kj // CHUNK
    in_band = (dchunk >= 0) & (dchunk <= LEFT_CHUNKS)
    return jnp.where(in_band[None], bias, NEG)


def _attn(proj, bias, gain, *, batch, seq):
    n = batch * seq
    w = ATTN_WIDTH
    steps = seq // ATTN_QBLK
    qcol = 4 * HGRN_WIDTH // w
    return pl.pallas_call(
        _attn_kernel,
        grid=(batch, steps),
        in_specs=[
            pl.BlockSpec((ATTN_QBLK, w), lambda b, i: (b * steps + i, qcol)),
            pl.BlockSpec((seq, w), lambda b, i: (b, qcol + 1)),
            pl.BlockSpec((seq, w), lambda b, i: (b, qcol + 2)),
            pl.BlockSpec((ATTN_HEADS, ATTN_QBLK, ATTN_WIN), lambda b, i: (0, 0, 0)),
            pl.BlockSpec((1, w), lambda b, i: (0, 0)),
        ],
        out_specs=pl.BlockSpec((ATTN_QBLK, w), lambda b, i: (b * steps + i, 0)),
        out_shape=jax.ShapeDtypeStruct((n, w), BF16),
        compiler_params=_params(("parallel", "arbitrary")),
        name="attn",
    )(proj, proj, proj, bias, gain)


def _post_kernel(h_ref, yh_ref, ya_ref, wout_ref, gq_ref, wq_ref, kv_ref, wo_ref, o_ref):
    d = h_ref.shape[1]
    hw = yh_ref.shape[1]
    h1 = (h_ref[...]
          + jnp.dot(yh_ref[...], wout_ref[0:hw, :], preferred_element_type=F32)
          + jnp.dot(ya_ref[...], wout_ref[hw:, :], preferred_element_type=F32))
    hn = _rms(h1, gq_ref[...]).astype(BF16)
    dh = d // MEM_HEADS
    q = (jnp.dot(hn, wq_ref[...], preferred_element_type=F32) * (dh ** -0.5)).astype(BF16)
    outs = []
    for hd in range(MEM_HEADS):
        sl = slice(hd * dh, (hd + 1) * dh)
        k = kv_ref[:, sl]
        v = kv_ref[:, d + hd * dh:d + (hd + 1) * dh]
        s = lax.dot_general(q[:, sl], k, (((1,), (1,)), ((), ())), preferred_element_type=F32)
        e, l = _softmax_rows(s)
        outs.append((jnp.dot(e.astype(BF16), v, preferred_element_type=F32) / l).astype(BF16))
    o = jnp.concatenate(outs, axis=1)
    o_ref[...] = h1 + jnp.dot(o, wo_ref[...], preferred_element_type=F32)


def _post(h, yh, ya, w_out, gq, wq, kv, wo, *, seq, mem_len, tm):
    n, d = h.shape
    per_batch = seq // tm
    const = lambda i: (0, 0)
    return pl.pallas_call(
        _post_kernel,
        grid=(n // tm,),
        in_specs=[
            pl.BlockSpec((tm, d), lambda i: (i, 0)),
            pl.BlockSpec((tm, yh.shape[1]), lambda i: (i, 0)),
            pl.BlockSpec((tm, ya.shape[1]), lambda i: (i, 0)),
            pl.BlockSpec(w_out.shape, const),
            pl.BlockSpec((1, d), const),
            pl.BlockSpec(wq.shape, const),
            pl.BlockSpec((mem_len, 2 * d), lambda i: (i // per_batch, 0)),
            pl.BlockSpec(wo.shape, const),
        ],
        out_specs=pl.BlockSpec((tm, d), lambda i: (i, 0)),
        out_shape=jax.ShapeDtypeStruct((n, d), F32),
        compiler_params=_params(("parallel",)),
        name="post",
    )(h, yh, ya, w_out, gq, wq, kv, wo)


def _ffn_kernel(h_ref, g_ref, wg_ref, wu_ref, wd_ref, o_ref, xn_scr, acc_scr):
    f = pl.program_id(1)

    @pl.when(f == 0)
    def _():
        xn_scr[...] = _rms(h_ref[...], g_ref[...]).astype(BF16)

    xn = xn_scr[...]
    a = jnp.dot(xn, wg_ref[...], preferred_element_type=F32)
    u = jnp.dot(xn, wu_ref[...], preferred_element_type=F32)
    act = (jax.nn.silu(a) * u).astype(BF16)
    part = jnp.dot(act, wd_ref[...], preferred_element_type=F32)

    @pl.when(f == 0)
    def _():
        acc_scr[...] = part

    @pl.when(f > 0)
    def _():
        acc_scr[...] += part

    @pl.when(f == pl.num_programs(1) - 1)
    def _():
        o_ref[...] = h_ref[...] + acc_scr[...]


def _ffn(h, gain, wg, wu, wd, *, tm, tf):
    n, d = h.shape
    dff = wg.shape[1]
    return pl.pallas_call(
        _ffn_kernel,
        grid=(n // tm, dff // tf),
        in_specs=[
            pl.BlockSpec((tm, d), lambda i, f: (i, 0)),
            pl.BlockSpec((1, d), lambda i, f: (0, 0)),
            pl.BlockSpec((d, tf), lambda i, f: (0, f)),
            pl.BlockSpec((d, tf), lambda i, f: (0, f)),
            pl.BlockSpec((tf, d), lambda i, f: (f, 0)),
        ],
        out_specs=pl.BlockSpec((tm, d), lambda i, f: (i, 0)),
        out_shape=jax.ShapeDtypeStruct((n, d), F32),
        scratch_shapes=[pltpu.VMEM((tm, d), BF16), pltpu.VMEM((tm, d), F32)],
        compiler_params=_params(("parallel", "arbitrary")),
        name="ffn",
    )(h, gain, wg, wu, wd)


def _router_kernel(h_ref, g_ref, w1_ref, w2_ref, tri_ref, ri_ref, rg_ref, cnt_ref, carry_scr):
    i = pl.program_id(0)
    tm = h_ref.shape[0]

    @pl.when(i == 0)
    def _():
        carry_scr[...] = jnp.zeros_like(carry_scr)

    hn = _rms(h_ref[...], g_ref[...])
    a1 = hn.astype(BF16)
    a2 = (hn - a1.astype(F32)).astype(BF16)
    lg = (jnp.dot(a1, w1_ref[...], preferred_element_type=F32)
          + jnp.dot(a1, w2_ref[...], preferred_element_type=F32)
          + jnp.dot(a2, w1_ref[...], preferred_element_type=F32))
    lgt = lg.T[0:N_EXPERTS, :]
    ids = lax.broadcasted_iota(jnp.int32, (N_EXPERTS, tm), 0)
    m1 = jnp.max(lgt, axis=0, keepdims=True)
    i1 = jnp.min(jnp.where(lgt == m1, ids, N_EXPERTS), axis=0, keepdims=True)
    rest = jnp.where(ids == i1, -jnp.inf, lgt)
    m2 = jnp.max(rest, axis=0, keepdims=True)
    i2 = jnp.min(jnp.where(rest == m2, ids, N_EXPERTS), axis=0, keepdims=True)
    e = jnp.exp(m2 - m1)
    g1 = 1.0 / (1.0 + e)
    g2 = e * g1
    hit1 = ids == i1
    hit2 = ids == i2
    onehot = jnp.where(hit1 | hit2, 1.0, 0.0)
    before = jnp.dot(onehot.astype(BF16), tri_ref[...], preferred_element_type=F32) + carry_scr[...][:, 0:1]
    r1 = jnp.sum(jnp.where(hit1, before, 0.0), axis=0, keepdims=True)
    r2 = jnp.sum(jnp.where(hit2, before, 0.0), axis=0, keepdims=True)
    carry_scr[...] = carry_scr[...] + jnp.sum(onehot, axis=1, keepdims=True)
    cnt_ref[...] = carry_scr[...]
    row = lax.broadcasted_iota(jnp.int32, (8, tm), 0)
    ri_ref[...] = jnp.where(row == 0, i1, jnp.where(row == 1, i2, jnp.where(
        row == 2, r1.astype(jnp.int32), jnp.where(row == 3, r2.astype(jnp.int32), 0))))
    rg_ref[...] = jnp.where(row == 0, g1, jnp.where(row == 1, g2, 0.0))


def _router(h, gain, w1, w2, tri, *, tm):
    n, d = h.shape
    return pl.pallas_call(
        _router_kernel,
        grid=(n // tm,),
        in_specs=[
            pl.BlockSpec((tm, d), lambda i: (i, 0)),
            pl.BlockSpec((1, d), lambda i: (0, 0)),
            pl.BlockSpec(w1.shape, lambda i: (0, 0)),
            pl.BlockSpec(w2.shape, lambda i: (0, 0)),
            pl.BlockSpec((tm, tm), lambda i: (0, 0)),
        ],
        out_specs=[
            pl.BlockSpec((8, tm), lambda i: (0, i)),
            pl.BlockSpec((8, tm), lambda i: (0, i)),
            pl.BlockSpec((N_EXPERTS, 128), lambda i: (0, 0)),
        ],
        out_shape=[
            jax.ShapeDtypeStruct((8, n), jnp.int32),
            jax.ShapeDtypeStruct((8, n), F32),
            jax.ShapeDtypeStruct((N_EXPERTS, 128), F32),
        ],
        scratch_shapes=[pltpu.VMEM((N_EXPERTS, 128), F32)],
        compiler_params=_params(("arbitrary",)),
        name="router",
    )(h, gain, w1, w2, tri)


def _row_copy(src, dst, sem):
    return pltpu.make_async_copy(src, dst, sem)


def _dispatch_kernel(dest_ref, h_ref, g_ref, xs_in_ref, xs_ref, idx_smem, buf, sem, isem):
    del xs_in_ref
    i = pl.program_id(0)
    tm = h_ref.shape[0]
    icopy = _row_copy(dest_ref.at[i], idx_smem, isem)
    icopy.start()
    buf[...] = _rms(h_ref[...], g_ref[...])
    icopy.wait()

    def issue(t, carry):
        for k in range(TOP_K):
            _row_copy(buf.at[pl.ds(t, 1)], xs_ref.at[pl.ds(idx_smem[k, t], 1)], sem).start()
        return carry

    lax.fori_loop(0, tm, issue, 0)

    def drain(t, carry):
        for k in range(TOP_K):
            _row_copy(buf.at[pl.ds(t, 1)], xs_ref.at[pl.ds(idx_smem[k, t], 1)], sem).wait()
        return carry

    lax.fori_loop(0, tm, drain, 0)


def _dispatch(dest, h, gain, xs0, *, tm):
    n, d = h.shape
    return pl.pallas_call(
        _dispatch_kernel,
        grid=(n // tm,),
        in_specs=[
            pl.BlockSpec(memory_space=pl.ANY),
            pl.BlockSpec((tm, d), lambda i: (i, 0)),
            pl.BlockSpec((1, d), lambda i: (0, 0)),
            pl.BlockSpec(memory_space=pl.ANY),
        ],
        out_specs=pl.BlockSpec(memory_space=pl.ANY),
        out_shape=jax.ShapeDtypeStruct(xs0.shape, xs0.dtype),
        scratch_shapes=[
            pltpu.SMEM((TOP_K, tm), jnp.int32),
            pltpu.VMEM((tm, d), F32),
            pltpu.SemaphoreType.DMA,
            pltpu.SemaphoreType.DMA,
        ],
        input_output_aliases={3: 0},
        compiler_params=_params(("arbitrary",)),
        name="dispatch",
    )(dest, h, gain, xs0)


def _experts_kernel(be_ref, nu_ref, x_ref, wg_ref, wu_ref, wd_ref, o_ref, xb_scr, acc_scr):
    j = pl.program_id(0)
    f = pl.program_id(1)

    @pl.when(j < nu_ref[0])
    def _():
        @pl.when(f == 0)
        def _():
            xb_scr[...] = x_ref[...].astype(BF16)

        xb = xb_scr[...]
        a = jnp.dot(xb, wg_ref[...], preferred_element_type=F32)
        u = jnp.dot(xb, wu_ref[...], preferred_element_type=F32)
        act = (jax.nn.silu(a) * u).astype(BF16)
        part = jnp.dot(act, wd_ref[...], preferred_element_type=F32)

        @pl.when(f == 0)
        def _():
            acc_scr[...] = part

        @pl.when(f > 0)
        def _():
            acc_scr[...] += part

        @pl.when(f == pl.num_programs(1) - 1)
        def _():
            o_ref[...] = acc_scr[...]

    @pl.when((j >= nu_ref[0]) & (f == pl.num_programs(1) - 1))
    def _():
        o_ref[...] = jnp.zeros_like(o_ref)


def _experts(block_expert, n_used, xs, wg, wu, wd, *, tf):
    p, d = xs.shape
    dff = wg.shape[2]
    nf = dff // tf
    n_blocks = p // MOE_ROWS

    def row(j, f, be, nu):
        return (jnp.minimum(j, nu[0] - 1), 0)

    def fcol(j, f, nu):
        return jnp.where(j < nu[0], f, nf - 1)

    grid_spec = pltpu.PrefetchScalarGridSpec(
        num_scalar_prefetch=2,
        grid=(n_blocks, nf),
        in_specs=[
            pl.BlockSpec((MOE_ROWS, d), row),
            pl.BlockSpec((None, d, tf), lambda j, f, be, nu: (be[j], 0, fcol(j, f, nu))),
            pl.BlockSpec((None, d, tf), lambda j, f, be, nu: (be[j], 0, fcol(j, f, nu))),
            pl.BlockSpec((None, tf, d), lambda j, f, be, nu: (be[j], fcol(j, f, nu), 0)),
        ],
        out_specs=pl.BlockSpec((MOE_ROWS, d), lambda j, f, be, nu: (j, 0)),
        scratch_shapes=[pltpu.VMEM((MOE_ROWS, d), BF16), pltpu.VMEM((MOE_ROWS, d), F32)],
    )
    return pl.pallas_call(
        _experts_kernel,
        grid_spec=grid_spec,
        out_shape=jax.ShapeDtypeStruct((p, d), F32),
        compiler_params=_params(("arbitrary", "arbitrary")),
        name="experts",
    )(block_expert, n_used, xs, wg, wu, wd)


def _combine_kernel(dest_ref, h_ref, rg_ref, gf_ref, yb_ref, o_ref, idx_smem, r_buf, sem, isem,
                    *, final_norm):
    i = pl.program_id(0)
    tm = h_ref.shape[0]
    icopy = _row_copy(dest_ref.at[i], idx_smem, isem)
    icopy.start()
    icopy.wait()

    def issue(t, carry):
        for k in range(TOP_K):
            _row_copy(yb_ref.at[pl.ds(idx_smem[k, t], 1)], r_buf.at[k, pl.ds(t, 1)], sem).start()
        return carry

    lax.fori_loop(0, tm, issue, 0)
    gates = rg_ref[...].T

    def drain(t, carry):
        for k in range(TOP_K):
            _row_copy(yb_ref.at[pl.ds(idx_smem[k, t], 1)], r_buf.at[k, pl.ds(t, 1)], sem).wait()
        return carry

    lax.fori_loop(0, tm, drain, 0)
    out = h_ref[...] + gates[:, 0:1] * r_buf[0] + gates[:, 1:2] * r_buf[1]
    if final_norm:
        out = _rms(out, gf_ref[...])
    o_ref[...] = out


def _combine(dest, h, rg, gain_final, yb, *, tm, final_norm):
    n, d = h.shape
    return pl.pallas_call(
        functools.partial(_combine_kernel, final_norm=final_norm),
        grid=(n // tm,),
        in_specs=[
            pl.BlockSpec(memory_space=pl.ANY),
            pl.BlockSpec((tm, d), lambda i: (i, 0)),
            pl.BlockSpec((8, tm), lambda i: (0, i)),
            pl.BlockSpec((1, d), lambda i: (0, 0)),
            pl.BlockSpec(memory_space=pl.ANY),
        ],
        out_specs=pl.BlockSpec((tm, d), lambda i: (i, 0)),
        out_shape=jax.ShapeDtypeStruct((n, d), F32),
        scratch_shapes=[
            pltpu.SMEM((TOP_K, tm), jnp.int32),
            pltpu.VMEM((TOP_K, tm, d), F32),
            pltpu.SemaphoreType.DMA,
            pltpu.SemaphoreType.DMA,
        ],
        compiler_params=_params(("arbitrary",)),
        name="combine",
    )(dest, h, rg, gain_final, yb)


def _moe(h, gain, w_router, wg, wu, wd, gain_final, *, final_norm, tm_route, tm_rows):
    n, d = h.shape
    wr = jnp.zeros((d, 128), F32).at[:, :N_EXPERTS].set(w_router.astype(F32))
    w1 = wr.astype(BF16)
    w2 = (wr - w1.astype(F32)).astype(BF16)
    tri = jnp.asarray(np.triu(np.ones((tm_route, tm_route), np.float32), 1), BF16)
    ri, rg, cnt = _router(h, gain, w1, w2, tri, tm=tm_route)

    counts = cnt[:, 0].astype(jnp.int32)
    padded = (counts + MOE_ROWS - 1) // MOE_ROWS * MOE_ROWS
    padded_end = jnp.cumsum(padded)
    padded_start = padded_end - padded
    n_blocks = -(-(n * TOP_K + N_EXPERTS * (MOE_ROWS - 1)) // MOE_ROWS)
    block_start = jnp.arange(n_blocks, dtype=jnp.int32) * MOE_ROWS
    block_expert = jnp.minimum(
        jnp.sum(block_start[:, None] >= padded_end[None, :], axis=1), N_EXPERTS - 1).astype(jnp.int32)
    n_used = (padded_end[-1:] // MOE_ROWS).astype(jnp.int32)
    onehot = ri[0:TOP_K, :, None] == jnp.arange(N_EXPERTS, dtype=jnp.int32)
    dest = jnp.sum(jnp.where(onehot, padded_start, 0), axis=-1) + ri[TOP_K:2 * TOP_K]
    dest = dest.astype(jnp.int32).reshape(TOP_K, n // tm_rows, tm_rows).transpose(1, 0, 2)

    xs0 = jnp.zeros((n_blocks * MOE_ROWS, d), F32)
    xs = _dispatch(dest, h, gain, xs0, tm=tm_rows)
    yb = _experts(block_expert, n_used, xs, wg, wu, wd, tf=512)
    return _combine(dest, h, rg, gain_final, yb, tm=tm_rows, final_norm=final_norm)


def kernel(x, mem, w_in, hgrn_lower_bound, hgrn_out_gain, attn_rel_bias, attn_out_gain, w_out,
           norm_mix, norm_mem_q, norm_mem_kv, w_mem_q, w_mem_kv, w_mem_o, norm_ffn,
           w_ffn_gate, w_ffn_up, w_ffn_down, w_router, w_exp_gate, w_exp_up, w_exp_down, norm_final):
    batch, seq, d = x.shape
    mem_len = mem.shape[1]
    depth = w_in.shape[0]
    n = batch * seq
    assert seq % 512 == 0 and seq >= ATTN_WIN and n % 1024 == 0

    lb_probs = jax.nn.softmax(hgrn_lower_bound.astype(F32), axis=0)
    lower_bounds = jnp.cumsum(lb_probs, axis=0) - lb_probs[0]

    h = x.reshape(n, d)
    mem2 = mem.reshape(batch * mem_len, d)
    row = lambda v: v.reshape(1, -1).astype(F32)
    for layer in range(depth):
        proj = _norm_matmul(h, row(norm_mix[layer]), w_in[layer].astype(BF16),
                            tm=512, col_chunk=512, name="in_proj")
        y_hgrn = _hgrn(proj, row(lower_bounds[layer]), row(hgrn_out_gain[layer]),
                       batch=batch, seq=seq, ts=512)
        y_attn = _attn(proj, _attn_bias(attn_rel_bias[layer]), row(attn_out_gain[layer]),
                       batch=batch, seq=seq)
        kv = _norm_matmul(mem2, row(norm_mem_kv[layer]), w_mem_kv[layer].astype(BF16),
                          tm=min(512, batch * mem_len), col_chunk=512, name="mem_kv")
        h = _post(h, y_hgrn, y_attn, w_out[layer].astype(BF16), row(norm_mem_q[layer]),
                  w_mem_q[layer].astype(BF16), kv, w_mem_o[layer].astype(BF16),
                  seq=seq, mem_len=mem_len, tm=512)
        j = layer // 2
        if layer % 2 == 0:
            h = _ffn(h, row(norm_ffn[layer]), w_ffn_gate[j].astype(BF16), w_ffn_up[j].astype(BF16),
                     w_ffn_down[j].astype(BF16), tm=1024, tf=256)
        else:
            last = layer == depth - 1
            h = _moe(h, row(norm_ffn[layer]), w_router[j], w_exp_gate[j].astype(BF16),
                     w_exp_up[j].astype(BF16), w_exp_down[j].astype(BF16), row(norm_final),
                     final_norm=last, tm_route=512, tm_rows=256)
    if depth % 2 == 1:
        h = _norm_only(h, row(norm_final))
    return h.reshape(batch, seq, d)


def _norm_only_kernel(x_ref, g_ref, o_ref):
    o_ref[...] = _rms(x_ref[...], g_ref[...])


def _norm_only(x, gain, tm=1024):
    n, d = x.shape
    return pl.pallas_call(
        _norm_only_kernel,
        grid=(n // tm,),
        in_specs=[pl.BlockSpec((tm, d), lambda i: (i, 0)), pl.BlockSpec((1, d), lambda i: (0, 0))],
        out_specs=pl.BlockSpec((tm, d), lambda i: (i, 0)),
        out_shape=jax.ShapeDtypeStruct((n, d), F32),
        compiler_params=_params(("parallel",)),
        name="final_norm",
    )(x, gain)
```

```python
import functools

import numpy as np
import jax
import jax.numpy as jnp
from jax import lax
from jax.experimental import pallas as pl
from jax.experimental.pallas import tpu as pltpu

F32 = jnp.float32
BF16 = jnp.bfloat16

EPS = 1e-6
CHUNK = 64
LEFT_CHUNKS = 8
MAX_REL = 128
HGRN_HEADS = 4
HGRN_HEAD_DIM = 128
HGRN_WIDTH = HGRN_HEADS * HGRN_HEAD_DIM
ATTN_HEADS = 8
ATTN_HEAD_DIM = 64
ATTN_WIDTH = ATTN_HEADS * ATTN_HEAD_DIM
MEM_HEADS = 4
N_EXPERTS = 8
TOP_K = 2

SUB = 16
FAST = 32
FAST_MAX_DECAY = 60.0
ATTN_QBLK = 4 * CHUNK
ATTN_WIN = ATTN_QBLK + LEFT_CHUNKS * CHUNK
MOE_STEPS = 7
MOE_ROWS = 144 * MOE_STEPS
NEG = -1e30

VMEM_LIMIT = 56 * 1024 * 1024


def _params(sem):
    return pltpu.CompilerParams(dimension_semantics=sem, vmem_limit_bytes=VMEM_LIMIT)


def _rms(x, gain):
    ms = jnp.mean(x * x, axis=-1, keepdims=True)
    return x * lax.rsqrt(ms + EPS) * gain


def _softmax_rows(s):
    m = jnp.max(s, axis=-1, keepdims=True)
    e = jnp.exp(s - m)
    return e, jnp.sum(e, axis=-1, keepdims=True)


def _norm_matmul_kernel(x_ref, g_ref, w_ref, o_ref, *, col_chunk):
    xn = _rms(x_ref[...], g_ref[...]).astype(BF16)
    n_out = o_ref.shape[1]
    for c in range(n_out // col_chunk):
        sl = slice(c * col_chunk, (c + 1) * col_chunk)
        o_ref[:, sl] = jnp.dot(xn, w_ref[:, sl], preferred_element_type=F32).astype(o_ref.dtype)


def _norm_matmul(x, gain, w, *, tm, col_chunk, name):
    n, d = x.shape
    n_out = w.shape[1]
    return pl.pallas_call(
        functools.partial(_norm_matmul_kernel, col_chunk=col_chunk),
        grid=(n // tm,),
        in_specs=[
            pl.BlockSpec((tm, d), lambda i: (i, 0)),
            pl.BlockSpec((1, d), lambda i: (0, 0)),
            pl.BlockSpec((d, n_out), lambda i: (0, 0)),
        ],
        out_specs=pl.BlockSpec((tm, n_out), lambda i: (i, 0)),
        out_shape=jax.ShapeDtypeStruct((n, n_out), BF16),
        compiler_params=_params(("parallel",)),
        name=name,
    )(x, gain, w)


def _hgrn_exact_tile(q_ref, i_ref, st_ref, lf_scr, k_scr, bl_scr, kc_scr, vc_scr, o_scr, ts):
    hd = HGRN_HEAD_DIM
    r = lax.broadcasted_iota(jnp.int32, (SUB, SUB), 0)
    c = lax.broadcasted_iota(jnp.int32, (SUB, SUB), 1)
    ltri = jnp.where(r >= c, 1.0, 0.0).astype(BF16)
    rowid = lax.broadcasted_iota(jnp.int32, (SUB, hd), 0)

    def sub_chunk(j, carry):
        r0 = pl.multiple_of(j * SUB, SUB)
        lf = lf_scr[pl.ds(r0, SUB), :]
        hi = lf.astype(BF16)
        lo = (lf - hi.astype(F32)).astype(BF16)
        bl = (jnp.dot(ltri, hi, preferred_element_type=F32)
              + jnp.dot(ltri, lo, preferred_element_type=F32))
        bl_scr[...] = bl
        kc_scr[...] = k_scr[pl.ds(r0, SUB), :]
        vc_scr[...] = i_ref[pl.ds(r0, SUB), :].astype(F32)
        qf = q_ref[pl.ds(r0, SUB), :].astype(F32)
        for h in range(HGRN_HEADS):
            sl = slice(h * hd, (h + 1) * hd)
            blh = bl[:, sl]
            qh = qf[:, sl]
            st = st_ref[h]
            o = lax.dot_general((qh * jnp.exp(blh)).astype(BF16), st.astype(BF16),
                                (((1,), (1,)), ((), ())), preferred_element_type=F32)
            for s in range(SUB):
                bs = bl_scr[s:s + 1, sl]
                ks = kc_scr[s:s + 1, sl]
                vs = vc_scr[s:s + 1, sl]
                e = jnp.exp(jnp.minimum(blh - bs, 0.0))
                a = jnp.sum(qh * e * ks, axis=-1, keepdims=True)
                a = jnp.where(rowid >= s, a, 0.0)
                o = o + a * vs
            o_scr[pl.ds(r0, SUB), sl] = o
            bend = bl_scr[SUB - 1:SUB, sl]
            kdec = (kc_scr[:, sl] * jnp.exp(bend - blh)).astype(BF16)
            vt = vc_scr[:, sl].T.astype(BF16)
            st_ref[h] = st * jnp.exp(bend) + jnp.dot(vt, kdec, preferred_element_type=F32)
        return carry

    lax.fori_loop(0, ts // SUB, sub_chunk, 0)


def _hgrn_fast_tile(q_ref, i_ref, st_ref, k_scr, b_scr, o_scr, ts):
    hd = HGRN_HEAD_DIM
    r = lax.broadcasted_iota(jnp.int32, (CHUNK, CHUNK), 0)
    c = lax.broadcasted_iota(jnp.int32, (CHUNK, CHUNK), 1)
    causal = r >= c
    first_half = lax.broadcasted_iota(jnp.int32, (CHUNK, HGRN_WIDTH), 0) < FAST

    def block(j, carry):
        r0 = pl.multiple_of(j * CHUNK, CHUNK)
        b = b_scr[pl.ds(r0, CHUNK), :]
        b_mid = b[FAST - 1:FAST, :]
        rel = b - jnp.where(first_half, b_mid, 0.0)
        e_mid = jnp.exp(b_mid)
        e_tail = jnp.exp(b[CHUNK - 1:CHUNK, :])
        kdf = k_scr[pl.ds(r0, CHUNK), :] * jnp.exp(-rel)
        qd = (q_ref[pl.ds(r0, CHUNK), :].astype(F32) * jnp.exp(rel)).astype(BF16)
        kd = kdf.astype(BF16)
        v = i_ref[pl.ds(r0, CHUNK), :]
        kdec = (kdf * e_tail).astype(BF16)
        vt = v.astype(F32).T.astype(BF16)
        for h in range(HGRN_HEADS):
            sl = slice(h * hd, (h + 1) * hd)
            st = st_ref[h]
            a = lax.dot_general(qd[:, sl], kd[:, sl], (((1,), (1,)), ((), ())),
                                preferred_element_type=F32)
            a = jnp.where(causal, a, 0.0).astype(BF16)
            st_mid = (st * e_mid[:, sl]).astype(BF16)
            o_scr[pl.ds(r0, CHUNK), sl] = (
                jnp.dot(a, v[:, sl], preferred_element_type=F32)
                + lax.dot_general(qd[:, sl], st_mid, (((1,), (1,)), ((), ())),
                                  preferred_element_type=F32))
            st_ref[h] = (st * (e_mid[:, sl] * e_tail[:, sl])
                         + jnp.dot(vt[sl, :], kdec[:, sl], preferred_element_type=F32))
        return carry

    lax.fori_loop(0, ts // CHUNK, block, 0, unroll=True)


def _hgrn_kernel(q_ref, f_ref, i_ref, g_ref, lb_ref, gain_ref, tri_ref, o_ref,
                 st_ref, lf_scr, k_scr, b_scr, bl_scr, kc_scr, vc_scr, o_scr, *, ts):
    hd = HGRN_HEAD_DIM

    @pl.when(pl.program_id(1) == 0)
    def _():
        st_ref[...] = jnp.zeros_like(st_ref)

    lb = lb_ref[...]
    f = lb + (1.0 - lb) * jax.nn.sigmoid(f_ref[...].astype(F32))
    lf = jnp.log(f)
    lf_scr[...] = lf
    k_scr[...] = 1.0 - f
    hi = lf.astype(BF16)
    lo = (lf - hi.astype(F32)).astype(BF16)
    b = (jnp.dot(tri_ref[...], hi, preferred_element_type=F32)
         + jnp.dot(tri_ref[...], lo, preferred_element_type=F32))
    b_scr[...] = b
    mild = jnp.min(b) >= -FAST_MAX_DECAY

    @pl.when(mild)
    def _():
        _hgrn_fast_tile(q_ref, i_ref, st_ref, k_scr, b_scr, o_scr, ts)

    @pl.when(jnp.logical_not(mild))
    def _():
        _hgrn_exact_tile(q_ref, i_ref, st_ref, lf_scr, k_scr, bl_scr, kc_scr, vc_scr, o_scr, ts)

    gate = jax.nn.silu(g_ref[...].astype(F32)) * gain_ref[...]
    for h in range(HGRN_HEADS):
        sl = slice(h * hd, (h + 1) * hd)
        o = o_scr[:, sl]
        o = o * lax.rsqrt(jnp.mean(o * o, axis=-1, keepdims=True) + EPS)
        o_ref[:, sl] = (o * gate[:, sl]).astype(o_ref.dtype)


def _hgrn(proj, lb, gain, *, batch, seq, ts):
    n = batch * seq
    w = HGRN_WIDTH
    steps = seq // ts
    rows = np.arange(ts)
    tri = jnp.asarray((rows[:, None] // FAST == rows[None, :] // FAST) & (rows[:, None] >= rows[None, :]), BF16)

    def col(j):
        return pl.BlockSpec((ts, w), lambda b, i, j=j: (b * steps + i, j))

    vec = pl.BlockSpec((1, w), lambda b, i: (0, 0))
    return pl.pallas_call(
        functools.partial(_hgrn_kernel, ts=ts),
        grid=(batch, steps),
        in_specs=[col(0), col(1), col(2), col(3), vec, vec, pl.BlockSpec((ts, ts), lambda b, i: (0, 0))],
        out_specs=pl.BlockSpec((ts, w), lambda b, i: (b * steps + i, 0)),
        out_shape=jax.ShapeDtypeStruct((n, w), BF16),
        scratch_shapes=[
            pltpu.VMEM((HGRN_HEADS, HGRN_HEAD_DIM, HGRN_HEAD_DIM), F32),
            pltpu.VMEM((ts, w), F32),
            pltpu.VMEM((ts, w), F32),
            pltpu.VMEM((ts, w), F32),
            pltpu.VMEM((SUB, w), F32),
            pltpu.VMEM((SUB, w), F32),
            pltpu.VMEM((SUB, w), F32),
            pltpu.VMEM((ts, w), F32),
        ],
        compiler_params=_params(("parallel", "arbitrary")),
        name="hgrn",
    )(proj, proj, proj, proj, lb, gain, tri)


def _attn_block(q, kwin, vwin, bias_ref, lo, gain, o_ref):
    qs = q * jnp.asarray(ATTN_HEAD_DIM ** -0.5, q.dtype)
    lane = lax.broadcasted_iota(jnp.int32, (ATTN_QBLK, 128), 1)
    outs = []
    for p in range(ATTN_HEADS // 2):
        sl = slice(p * 128, (p + 1) * 128)
        qp, kp, vp = qs[:, sl], kwin[:, sl], vwin[:, sl]
        o_pair = None
        for hh in range(2):
            sel = (lane < ATTN_HEAD_DIM) if hh == 0 else (lane >= ATTN_HEAD_DIM)
            qm = jnp.where(sel, qp, jnp.zeros_like(qp))
            s = lax.dot_general(qm, kp, (((1,), (1,)), ((), ())), preferred_element_type=F32)
            s = s + bias_ref[2 * p + hh, :, lo:]
            e, l = _softmax_rows(s)
            o = jnp.dot(e.astype(BF16), vp, preferred_element_type=F32) / l
            o_pair = o if o_pair is None else jnp.where(sel, o, o_pair)
        outs.append(o_pair)
    o = jnp.concatenate(outs, axis=1)
    o_ref[...] = _rms(o, gain).astype(o_ref.dtype)


def _attn_kernel(q_ref, k_ref, v_ref, bias_ref, gain_ref, o_ref):
    i = pl.program_id(1)
    gain = gain_ref[...]
    left = LEFT_CHUNKS * CHUNK

    for blk in range(left // ATTN_QBLK):
        @pl.when(i == blk)
        def _(blk=blk):
            w = (blk + 1) * ATTN_QBLK
            _attn_block(q_ref[...], k_ref[0:w, :], v_ref[0:w, :], bias_ref, ATTN_WIN - w, gain, o_ref)

    @pl.when(i >= left // ATTN_QBLK)
    def _():
        start = pl.multiple_of(i * ATTN_QBLK - left, ATTN_QBLK)
        _attn_block(q_ref[...], k_ref[pl.ds(start, ATTN_WIN), :], v_ref[pl.ds(start, ATTN_WIN), :],
                    bias_ref, 0, gain, o_ref)


def _attn_bias(rel_bias):
    q, w = ATTN_QBLK, ATTN_WIN
    span = q + w - 1
    r = LEFT_CHUNKS * CHUNK + (q - 1) - np.arange(span)
    vec = rel_bias.astype(F32)[:, np.clip(r, -MAX_REL, MAX_REL) + MAX_REL]
    skew = jnp.tile(vec, (1, q + 1))[:, :q * (span + 1)].reshape(-1, q, span + 1)
    bias = skew[:, ::-1, :w]
    qi = np.arange(q)[:, None]
    kj = np.arange(w)[None, :]
    dchunk = qi // CHUNK + LEFT_CHUNKS - kj // CHUNK
    in_band = (dchunk >= 0) & (dchunk <= LEFT_CHUNKS)
    return jnp.where(in_band[None], bias, NEG)


def _attn(proj, bias, gain, *, batch, seq):
    n = batch * seq
    w = ATTN_WIDTH
    steps = seq // ATTN_QBLK
    qcol = 4 * HGRN_WIDTH // w
    return pl.pallas_call(
        _attn_kernel,
        grid=(batch, steps),
        in_specs=[
            pl.BlockSpec((ATTN_QBLK, w), lambda b, i: (b * steps + i, qcol)),
            pl.BlockSpec((seq, w), lambda b, i: (b, qcol + 1)),
            pl.BlockSpec((seq, w), lambda b, i: (b, qcol + 2)),
            pl.BlockSpec((ATTN_HEADS, ATTN_QBLK, ATTN_WIN), lambda b, i: (0, 0, 0)),
            pl.BlockSpec((1, w), lambda b, i: (0, 0)),
        ],
        out_specs=pl.BlockSpec((ATTN_QBLK, w), lambda b, i: (b * steps + i, 0)),
        out_shape=jax.ShapeDtypeStruct((n, w), BF16),
        compiler_params=_params(("parallel", "arbitrary")),
        name="attn",
    )(proj, proj, proj, bias, gain)


def _post_kernel(h_ref, yh_ref, ya_ref, wout_ref, gq_ref, wq_ref, kv_ref, wo_ref, o_ref):
    d = h_ref.shape[1]
    hw = yh_ref.shape[1]
    h1 = (h_ref[...]
          + jnp.dot(yh_ref[...], wout_ref[0:hw, :], preferred_element_type=F32)
          + jnp.dot(ya_ref[...], wout_ref[hw:, :], preferred_element_type=F32))
    hn = _rms(h1, gq_ref[...]).astype(BF16)
    dh = d // MEM_HEADS
    q = (jnp.dot(hn, wq_ref[...], preferred_element_type=F32) * (dh ** -0.5)).astype(BF16)
    outs = []
    for hd in range(MEM_HEADS):
        sl = slice(hd * dh, (hd + 1) * dh)
        k = kv_ref[:, sl]
        v = kv_ref[:, d + hd * dh:d + (hd + 1) * dh]
        s = lax.dot_general(q[:, sl], k, (((1,), (1,)), ((), ())), preferred_element_type=F32)
        e, l = _softmax_rows(s)
        outs.append((jnp.dot(e.astype(BF16), v, preferred_element_type=F32) / l).astype(BF16))
    o = jnp.concatenate(outs, axis=1)
    o_ref[...] = h1 + jnp.dot(o, wo_ref[...], preferred_element_type=F32)


def _post(h, yh, ya, w_out, gq, wq, kv, wo, *, seq, mem_len, tm):
    n, d = h.shape
    per_batch = seq // tm
    const = lambda i: (0, 0)
    return pl.pallas_call(
        _post_kernel,
        grid=(n // tm,),
        in_specs=[
            pl.BlockSpec((tm, d), lambda i: (i, 0)),
            pl.BlockSpec((tm, yh.shape[1]), lambda i: (i, 0)),
            pl.BlockSpec((tm, ya.shape[1]), lambda i: (i, 0)),
            pl.BlockSpec(w_out.shape, const),
            pl.BlockSpec((1, d), const),
            pl.BlockSpec(wq.shape, const),
            pl.BlockSpec((mem_len, 2 * d), lambda i: (i // per_batch, 0)),
            pl.BlockSpec(wo.shape, const),
        ],
        out_specs=pl.BlockSpec((tm, d), lambda i: (i, 0)),
        out_shape=jax.ShapeDtypeStruct((n, d), F32),
        compiler_params=_params(("parallel",)),
        name="post",
    )(h, yh, ya, w_out, gq, wq, kv, wo)


def _ffn_kernel(h_ref, g_ref, wg_ref, wu_ref, wd_ref, o_ref, xn_scr, acc_scr):
    f = pl.program_id(1)

    @pl.when(f == 0)
    def _():
        xn_scr[...] = _rms(h_ref[...], g_ref[...]).astype(BF16)
        acc_scr[...] = jnp.zeros_like(acc_scr)

    xn = xn_scr[...]
    a = jnp.dot(xn, wg_ref[...], preferred_element_type=F32)
    u = jnp.dot(xn, wu_ref[...], preferred_element_type=F32)
    act = (jax.nn.silu(a) * u).astype(BF16)
    acc_scr[...] += jnp.dot(act, wd_ref[...], preferred_element_type=F32)

    @pl.when(f == pl.num_programs(1) - 1)
    def _():
        o_ref[...] = h_ref[...] + acc_scr[...]


def _ffn(h, gain, wg, wu, wd, *, tm, tf):
    n, d = h.shape
    dff = wg.shape[1]
    return pl.pallas_call(
        _ffn_kernel,
        grid=(n // tm, dff // tf),
        in_specs=[
            pl.BlockSpec((tm, d), lambda i, f: (i, 0)),
            pl.BlockSpec((1, d), lambda i, f: (0, 0)),
            pl.BlockSpec((d, tf), lambda i, f: (0, f)),
            pl.BlockSpec((d, tf), lambda i, f: (0, f)),
            pl.BlockSpec((tf, d), lambda i, f: (f, 0)),
        ],
        out_specs=pl.BlockSpec((tm, d), lambda i, f: (i, 0)),
        out_shape=jax.ShapeDtypeStruct((n, d), F32),
        scratch_shapes=[pltpu.VMEM((tm, d), BF16), pltpu.VMEM((tm, d), F32)],
        compiler_params=_params(("parallel", "arbitrary")),
        name="ffn",
    )(h, gain, wg, wu, wd)


def _router_kernel(h_ref, g_ref, w1_ref, w2_ref, tri_ref, ri_ref, rg_ref, cnt_ref, carry_scr):
    i = pl.program_id(0)
    tm = h_ref.shape[0]

    @pl.when(i == 0)
    def _():
        carry_scr[...] = jnp.zeros_like(carry_scr)

    hn = _rms(h_ref[...], g_ref[...])
    a1 = hn.astype(BF16)
    a2 = (hn - a1.astype(F32)).astype(BF16)
    lg = (jnp.dot(a1, w1_ref[...], preferred_element_type=F32)
          + jnp.dot(a1, w2_ref[...], preferred_element_type=F32)
          + jnp.dot(a2, w1_ref[...], preferred_element_type=F32))
    lgt = lg.T[0:N_EXPERTS, :]
    ids = lax.broadcasted_iota(jnp.int32, (N_EXPERTS, tm), 0)
    m1 = jnp.max(lgt, axis=0, keepdims=True)
    i1 = jnp.min(jnp.where(lgt == m1, ids, N_EXPERTS), axis=0, keepdims=True)
    rest = jnp.where(ids == i1, -jnp.inf, lgt)
    m2 = jnp.max(rest, axis=0, keepdims=True)
    i2 = jnp.min(jnp.where(rest == m2, ids, N_EXPERTS), axis=0, keepdims=True)
    e = jnp.exp(m2 - m1)
    g1 = 1.0 / (1.0 + e)
    g2 = e * g1
    hit1 = ids == i1
    hit2 = ids == i2
    onehot = jnp.where(hit1 | hit2, 1.0, 0.0)
    before = jnp.dot(onehot.astype(BF16), tri_ref[...], preferred_element_type=F32) + carry_scr[...][:, 0:1]
    r1 = jnp.sum(jnp.where(hit1, before, 0.0), axis=0, keepdims=True)
    r2 = jnp.sum(jnp.where(hit2, before, 0.0), axis=0, keepdims=True)
    carry_scr[...] = carry_scr[...] + jnp.sum(onehot, axis=1, keepdims=True)
    cnt_ref[...] = carry_scr[...]
    row = lax.broadcasted_iota(jnp.int32, (8, tm), 0)
    ri_ref[...] = jnp.where(row == 0, i1, jnp.where(row == 1, i2, jnp.where(
        row == 2, r1.astype(jnp.int32), jnp.where(row == 3, r2.astype(jnp.int32), 0))))
    rg_ref[...] = jnp.where(row == 0, g1, jnp.where(row == 1, g2, 0.0))


def _router(h, gain, w1, w2, tri, *, tm):
    n, d = h.shape
    return pl.pallas_call(
        _router_kernel,
        grid=(n // tm,),
        in_specs=[
            pl.BlockSpec((tm, d), lambda i: (i, 0)),
            pl.BlockSpec((1, d), lambda i: (0, 0)),
            pl.BlockSpec(w1.shape, lambda i: (0, 0)),
            pl.BlockSpec(w2.shape, lambda i: (0, 0)),
            pl.BlockSpec((tm, tm), lambda i: (0, 0)),
        ],
        out_specs=[
            pl.BlockSpec((8, tm), lambda i: (0, i)),
            pl.BlockSpec((8, tm), lambda i: (0, i)),
            pl.BlockSpec((N_EXPERTS, 128), lambda i: (0, 0)),
        ],
        out_shape=[
            jax.ShapeDtypeStruct((8, n), jnp.int32),
            jax.ShapeDtypeStruct((8, n), F32),
            jax.ShapeDtypeStruct((N_EXPERTS, 128), F32),
        ],
        scratch_shapes=[pltpu.VMEM((N_EXPERTS, 128), F32)],
        compiler_params=_params(("arbitrary",)),
        name="router",
    )(h, gain, w1, w2, tri)


def _experts_kernel(be_ref, nu_ref, idx_hbm, h_hbm, g_ref, wg_ref, wu_ref, wd_ref, y_hbm,
                    idx_smem, xbuf, obuf, xb_scr, isem, gsem, ssem, *, n_tokens):
    j = pl.program_id(0)
    f = pl.program_id(1)
    nf = pl.num_programs(1)
    nu = nu_ref[0]
    rows_per_step = MOE_ROWS // MOE_STEPS

    def idx_copy(b):
        s = lax.rem(b + 1, 4)
        return pltpu.make_async_copy(idx_hbm.at[b + 1], idx_smem.at[s], isem.at[s])

    def gather(b, step, u):
        tok = idx_smem[lax.rem(b + 1, 4), step, u]
        s = lax.rem(b + 2, 2)
        r = step * rows_per_step + u
        return pltpu.make_async_copy(h_hbm.at[pl.ds(tok, 1)], xbuf.at[s, pl.ds(r, 1)], gsem.at[s])

    def scatter(b, step, u):
        tgt = idx_smem[lax.rem(b + 1, 4), MOE_STEPS + step, u]
        s = lax.rem(b + 2, 2)
        r = step * rows_per_step + u
        return pltpu.make_async_copy(obuf.at[s, pl.ds(r, 1)], y_hbm.at[pl.ds(tgt, 1)], ssem.at[s])

    def for_rows(fn, steps=range(MOE_STEPS)):
        for step in steps:
            def body(u, carry, step=step):
                fn(step, u)
                return carry
            lax.fori_loop(0, rows_per_step, body, 0, unroll=16)

    @pl.when((j == 0) & (f == 0))
    def _():
        obuf[1] = jnp.zeros(obuf.shape[1:], obuf.dtype)
        for half in range(2):
            spare = pltpu.make_async_copy(
                obuf.at[1], y_hbm.at[pl.ds(TOP_K * n_tokens + half * MOE_ROWS, MOE_ROWS)], ssem.at[half])
            spare.start()
            spare.wait()
        for b in (-1, 0, 1):
            idx_copy(b).start()
        idx_copy(-1).wait()
        idx_copy(0).wait()
        for_rows(lambda step, u: gather(0, step, u).start())

    @pl.when((f == 0) & (j <= nu))
    def _():
        for_rows(lambda step, u: gather(j, step, u).wait())

        @pl.when(j >= 1)
        def _():
            for_rows(lambda step, u: scatter(j - 2, step, u).wait())

        idx_copy(j + 1).wait()

        @pl.when(j < nu)
        def _():
            idx_copy(j + 2).start()

    @pl.when(j < nu)
    def _():
        slot = lax.rem(j, 2)

        @pl.when(f == 0)
        def _():
            xb_scr[...] = _rms(xbuf[slot], g_ref[...]).astype(BF16)
            obuf[slot] = jnp.zeros(obuf.shape[1:], obuf.dtype)

        for u in range(rows_per_step):
            gather(j + 1, f, u).start()
            scatter(j - 1, f, u).start()
        xb = xb_scr[...]
        a = jnp.dot(xb, wg_ref[...], preferred_element_type=F32)
        u_ = jnp.dot(xb, wu_ref[...], preferred_element_type=F32)
        act = (jax.nn.silu(a) * u_).astype(BF16)
        obuf[slot] += jnp.dot(act, wd_ref[...], preferred_element_type=F32)

    @pl.when(j == nu)
    def _():
        for_rows(lambda step, u: scatter(j - 1, step, u).start(), steps=[f])

        @pl.when(f == nf - 1)
        def _():
            for_rows(lambda step, u: scatter(j - 1, step, u).wait())


def _experts(block_expert, n_used, idx, h, gain, wg, wu, wd):
    n, d = h.shape
    dff = wg.shape[2]
    tf = dff // MOE_STEPS
    n_blocks = idx.shape[0] - 3

    def fcol(j, f, nu):
        return jnp.where(j < nu[0], f, MOE_STEPS - 1)

    def eblk(j, be, nu):
        return be[jnp.minimum(j, nu[0] - 1)]

    grid_spec = pltpu.PrefetchScalarGridSpec(
        num_scalar_prefetch=2,
        grid=(n_blocks + 1, MOE_STEPS),
        in_specs=[
            pl.BlockSpec(memory_space=pl.ANY),
            pl.BlockSpec(memory_space=pl.ANY),
            pl.BlockSpec((1, d), lambda j, f, be, nu: (0, 0)),
            pl.BlockSpec((None, d, tf), lambda j, f, be, nu: (eblk(j, be, nu), 0, fcol(j, f, nu))),
            pl.BlockSpec((None, d, tf), lambda j, f, be, nu: (eblk(j, be, nu), 0, fcol(j, f, nu))),
            pl.BlockSpec((None, tf, d), lambda j, f, be, nu: (eblk(j, be, nu), fcol(j, f, nu), 0)),
        ],
        out_specs=pl.BlockSpec(memory_space=pl.ANY),
        scratch_shapes=[
            pltpu.SMEM((4, 2 * MOE_STEPS, MOE_ROWS // MOE_STEPS), jnp.int32),
            pltpu.VMEM((2, MOE_ROWS, d), F32),
            pltpu.VMEM((2, MOE_ROWS, d), F32),
            pltpu.VMEM((MOE_ROWS, d), BF16),
            pltpu.SemaphoreType.DMA((4,)),
            pltpu.SemaphoreType.DMA((2,)),
            pltpu.SemaphoreType.DMA((2,)),
        ],
    )
    return pl.pallas_call(
        functools.partial(_experts_kernel, n_tokens=n),
        grid_spec=grid_spec,
        out_shape=jax.ShapeDtypeStruct((TOP_K * n + 2 * MOE_ROWS, d), F32),
        compiler_params=_params(("arbitrary", "arbitrary")),
        name="experts",
    )(block_expert, n_used, idx, h, gain, wg, wu, wd)


def _combine_kernel(h_ref, rg_ref, gf_ref, y0_ref, y1_ref, o_ref, *, final_norm):
    gates = rg_ref[...].T
    out = h_ref[...] + gates[:, 0:1] * y0_ref[...] + gates[:, 1:2] * y1_ref[...]
    if final_norm:
        out = _rms(out, gf_ref[...])
    o_ref[...] = out


def _combine(h, rg, gain_final, y, *, tm, final_norm):
    n, d = h.shape
    return pl.pallas_call(
        functools.partial(_combine_kernel, final_norm=final_norm),
        grid=(n // tm,),
        in_specs=[
            pl.BlockSpec((tm, d), lambda i: (i, 0)),
            pl.BlockSpec((8, tm), lambda i: (0, i)),
            pl.BlockSpec((1, d), lambda i: (0, 0)),
            pl.BlockSpec((tm, d), lambda i: (i, 0)),
            pl.BlockSpec((tm, d), lambda i: (i + n // tm, 0)),
        ],
        out_specs=pl.BlockSpec((tm, d), lambda i: (i, 0)),
        out_shape=jax.ShapeDtypeStruct((n, d), F32),
        compiler_params=_params(("parallel",)),
        name="combine",
    )(h, rg, gain_final, y, y)


def _moe(h, gain, w_router, wg, wu, wd, gain_final, *, final_norm, tm_route, tm_rows):
    n, d = h.shape
    wr = jnp.zeros((d, 128), F32).at[:, :N_EXPERTS].set(w_router.astype(F32))
    w1 = wr.astype(BF16)
    w2 = (wr - w1.astype(F32)).astype(BF16)
    tri = jnp.asarray(np.triu(np.ones((tm_route, tm_route), np.float32), 1), BF16)
    ri, rg, cnt = _router(h, gain, w1, w2, tri, tm=tm_route)

    counts = cnt[:, 0].astype(jnp.int32)
    padded = (counts + MOE_ROWS - 1) // MOE_ROWS * MOE_ROWS
    padded_end = jnp.cumsum(padded)
    padded_start = padded_end - padded
    n_blocks = -(-(n * TOP_K + N_EXPERTS * (MOE_ROWS - 1)) // MOE_ROWS)
    block_start = jnp.arange(n_blocks, dtype=jnp.int32) * MOE_ROWS
    block_expert = jnp.minimum(
        jnp.sum(block_start[:, None] >= padded_end[None, :], axis=1), N_EXPERTS - 1).astype(jnp.int32)
    n_used = (padded_end[-1:] // MOE_ROWS).astype(jnp.int32)
    onehot = ri[0:TOP_K, :, None] == jnp.arange(N_EXPERTS, dtype=jnp.int32)
    dest = jnp.sum(jnp.where(onehot, padded_start, 0), axis=-1) + ri[TOP_K:2 * TOP_K]

    n_idx = (n_blocks + 3) * MOE_ROWS
    pos = jnp.arange(n_idx, dtype=jnp.int32)
    spare = TOP_K * n + (pos // MOE_ROWS % 2) * MOE_ROWS + pos % MOE_ROWS
    target = spare.at[dest.reshape(-1) + MOE_ROWS].set(
        jnp.arange(TOP_K * n, dtype=jnp.int32), unique_indices=True)
    token = jnp.where(target < TOP_K * n, target % n, 0)
    per_step = MOE_ROWS // MOE_STEPS
    idx = jnp.concatenate([token.reshape(-1, MOE_STEPS, per_step),
                           target.reshape(-1, MOE_STEPS, per_step)], axis=1)

    y = _experts(block_expert, n_used, idx, h, gain, wg, wu, wd)
    return _combine(h, rg, gain_final, y, tm=tm_rows, final_norm=final_norm)


def kernel(x, mem, w_in, hgrn_lower_bound, hgrn_out_gain, attn_rel_bias, attn_out_gain, w_out,
           norm_mix, norm_mem_q, norm_mem_kv, w_mem_q, w_mem_kv, w_mem_o, norm_ffn,
           w_ffn_gate, w_ffn_up, w_ffn_down, w_router, w_exp_gate, w_exp_up, w_exp_down, norm_final):
    batch, seq, d = x.shape
    mem_len = mem.shape[1]
    depth = w_in.shape[0]
    n = batch * seq
    assert seq % 512 == 0 and seq >= ATTN_WIN and n % 1024 == 0

    lb_probs = jax.nn.softmax(hgrn_lower_bound.astype(F32), axis=0)
    lower_bounds = jnp.cumsum(lb_probs, axis=0) - lb_probs[0]

    h = x.reshape(n, d)
    mem2 = mem.reshape(batch * mem_len, d)
    row = lambda v: v.reshape(1, -1).astype(F32)
    for layer in range(depth):
        proj = _norm_matmul(h, row(norm_mix[layer]), w_in[layer].astype(BF16),
                            tm=512, col_chunk=512, name="in_proj")
        y_hgrn = _hgrn(proj, row(lower_bounds[layer]), row(hgrn_out_gain[layer]),
                       batch=batch, seq=seq, ts=512)
        y_attn = _attn(proj, _attn_bias(attn_rel_bias[layer]), row(attn_out_gain[layer]),
                       batch=batch, seq=seq)
        kv = _norm_matmul(mem2, row(norm_mem_kv[layer]), w_mem_kv[layer].astype(BF16),
                          tm=min(512, batch * mem_len), col_chunk=512, name="mem_kv")
        h = _post(h, y_hgrn, y_attn, w_out[layer].astype(BF16), row(norm_mem_q[layer]),
                  w_mem_q[layer].astype(BF16), kv, w_mem_o[layer].astype(BF16),
                  seq=seq, mem_len=mem_len, tm=512)
        j = layer // 2
        if layer % 2 == 0:
            h = _ffn(h, row(norm_ffn[layer]), w_ffn_gate[j].astype(BF16), w_ffn_up[j].astype(BF16),
                     w_ffn_down[j].astype(BF16), tm=1024, tf=256)
        else:
            last = layer == depth - 1
            h = _moe(h, row(norm_ffn[layer]), w_router[j], w_exp_gate[j].astype(BF16),
                     w_exp_up[j].astype(BF16), w_exp_down[j].astype(BF16), row(norm_final),
                     final_norm=last, tm_route=512, tm_rows=256)
    if depth % 2 == 1:
        h = _norm_only(h, row(norm_final))
    return h.reshape(batch, seq, d)


def _norm_only_kernel(x_ref, g_ref, o_ref):
    o_ref[...] = _rms(x_ref[...], g_ref[...])


def _norm_only(x, gain, tm=1024):
    n, d = x.shape
    return pl.pallas_call(
        _norm_only_kernel,
        grid=(n // tm,),
        in_specs=[pl.BlockSpec((tm, d), lambda i: (i, 0)), pl.BlockSpec((1, d), lambda i: (0, 0))],
        out_specs=pl.BlockSpec((tm, d), lambda i: (i, 0)),
        out_shape=jax.ShapeDtypeStruct((n, d), F32),
        compiler_params=_params(("parallel",)),
        name="final_norm",
    )(x, gain)
```

```python
import functools

import numpy as np
import jax
import jax.numpy as jnp
from jax import lax
from jax.experimental import pallas as pl
from jax.experimental.pallas import tpu as pltpu

F32 = jnp.float32
BF16 = jnp.bfloat16

EPS = 1e-6
CHUNK = 64
LEFT_CHUNKS = 8
MAX_REL = 128
HGRN_HEADS = 4
HGRN_HEAD_DIM = 128
HGRN_WIDTH = HGRN_HEADS * HGRN_HEAD_DIM
ATTN_HEADS = 8
ATTN_HEAD_DIM = 64
ATTN_WIDTH = ATTN_HEADS * ATTN_HEAD_DIM
MEM_HEADS = 4
N_EXPERTS = 8
TOP_K = 2

SUB = 16
FAST = 32
FAST_MAX_DECAY = 60.0
ATTN_QBLK = 4 * CHUNK
ATTN_WIN = ATTN_QBLK + LEFT_CHUNKS * CHUNK
MOE_STEPS = 7
LANE_TILES = 8
MOE_ROWS = 144 * MOE_STEPS
NEG = -1e30
LOG2E = 1.4426950408889634

VMEM_LIMIT = 56 * 1024 * 1024


def _params(sem):
    return pltpu.CompilerParams(dimension_semantics=sem, vmem_limit_bytes=VMEM_LIMIT)


def _rms(x, gain):
    ms = jnp.mean(x * x, axis=-1, keepdims=True)
    return x * lax.rsqrt(ms + EPS) * gain


def _softmax_rows(s):
    m = jnp.max(s, axis=-1, keepdims=True)
    e = jnp.exp(s - m)
    return e, jnp.sum(e, axis=-1, keepdims=True)


def _norm_matmul_kernel(x_ref, g_ref, w_ref, o_ref, *, col_chunk):
    xn = _rms(x_ref[...], g_ref[...]).astype(BF16)
    n_out = o_ref.shape[1]
    for c in range(n_out // col_chunk):
        sl = slice(c * col_chunk, (c + 1) * col_chunk)
        o_ref[:, sl] = jnp.dot(xn, w_ref[:, sl], preferred_element_type=F32).astype(o_ref.dtype)


def _norm_matmul(x, gain, w, *, tm, col_chunk, name):
    n, d = x.shape
    n_out = w.shape[1]
    return pl.pallas_call(
        functools.partial(_norm_matmul_kernel, col_chunk=col_chunk),
        grid=(n // tm,),
        in_specs=[
            pl.BlockSpec((tm, d), lambda i: (i, 0)),
            pl.BlockSpec((1, d), lambda i: (0, 0)),
            pl.BlockSpec((d, n_out), lambda i: (0, 0)),
        ],
        out_specs=pl.BlockSpec((tm, n_out), lambda i: (i, 0)),
        out_shape=jax.ShapeDtypeStruct((n, n_out), BF16),
        compiler_params=_params(("parallel",)),
        name=name,
    )(x, gain, w)


def _hgrn_exact_tile(q_ref, i_ref, st_ref, lf_scr, k_scr, bl_scr, kc_scr, vc_scr, o_scr, ts):
    hd = HGRN_HEAD_DIM
    r = lax.broadcasted_iota(jnp.int32, (SUB, SUB), 0)
    c = lax.broadcasted_iota(jnp.int32, (SUB, SUB), 1)
    ltri = jnp.where(r >= c, 1.0, 0.0).astype(BF16)
    rowid = lax.broadcasted_iota(jnp.int32, (SUB, hd), 0)

    def sub_chunk(j, carry):
        r0 = pl.multiple_of(j * SUB, SUB)
        lf = lf_scr[pl.ds(r0, SUB), :]
        hi = lf.astype(BF16)
        lo = (lf - hi.astype(F32)).astype(BF16)
        bl = (jnp.dot(ltri, hi, preferred_element_type=F32)
              + jnp.dot(ltri, lo, preferred_element_type=F32))
        bl_scr[...] = bl
        kc_scr[...] = k_scr[pl.ds(r0, SUB), :]
        vc_scr[...] = i_ref[pl.ds(r0, SUB), :].astype(F32)
        qf = q_ref[pl.ds(r0, SUB), :].astype(F32)
        for h in range(HGRN_HEADS):
            sl = slice(h * hd, (h + 1) * hd)
            blh = bl[:, sl]
            qh = qf[:, sl]
            st = st_ref[h]
            o = lax.dot_general((qh * jnp.exp(blh)).astype(BF16), st.astype(BF16),
                                (((1,), (1,)), ((), ())), preferred_element_type=F32)
            for s in range(SUB):
                bs = bl_scr[s:s + 1, sl]
                ks = kc_scr[s:s + 1, sl]
                vs = vc_scr[s:s + 1, sl]
                e = jnp.exp(jnp.minimum(blh - bs, 0.0))
                a = jnp.sum(qh * e * ks, axis=-1, keepdims=True)
                a = jnp.where(rowid >= s, a, 0.0)
                o = o + a * vs
            o_scr[pl.ds(r0, SUB), sl] = o
            bend = bl_scr[SUB - 1:SUB, sl]
            kdec = (kc_scr[:, sl] * jnp.exp(bend - blh)).astype(BF16)
            vt = vc_scr[:, sl].T.astype(BF16)
            st_ref[h] = st * jnp.exp(bend) + jnp.dot(vt, kdec, preferred_element_type=F32)
        return carry

    lax.fori_loop(0, ts // SUB, sub_chunk, 0)


def _hgrn_fast_tile(q_ref, i_ref, st_ref, k_scr, b_scr, o_scr, ts):
    hd = HGRN_HEAD_DIM
    r = lax.broadcasted_iota(jnp.int32, (CHUNK, CHUNK), 0)
    c = lax.broadcasted_iota(jnp.int32, (CHUNK, CHUNK), 1)
    causal = r >= c
    first_half = lax.broadcasted_iota(jnp.int32, (CHUNK, HGRN_WIDTH), 0) < FAST

    def block(j, carry):
        r0 = pl.multiple_of(j * CHUNK, CHUNK)
        b = b_scr[pl.ds(r0, CHUNK), :]
        b_mid = b[FAST - 1:FAST, :]
        rel = b - jnp.where(first_half, b_mid, 0.0)
        e_mid = jnp.exp(b_mid)
        e_tail = jnp.exp(b[CHUNK - 1:CHUNK, :])
        kdf = k_scr[pl.ds(r0, CHUNK), :] * jnp.exp(-rel)
        qd = (q_ref[pl.ds(r0, CHUNK), :].astype(F32) * jnp.exp(rel)).astype(BF16)
        kd = kdf.astype(BF16)
        v = i_ref[pl.ds(r0, CHUNK), :]
        kdec = (kdf * e_tail).astype(BF16)
        vt = v.astype(F32).T.astype(BF16)
        for h in range(HGRN_HEADS):
            sl = slice(h * hd, (h + 1) * hd)
            st = st_ref[h]
            a = lax.dot_general(qd[:, sl], kd[:, sl], (((1,), (1,)), ((), ())),
                                preferred_element_type=F32)
            a = jnp.where(causal, a, 0.0).astype(BF16)
            st_mid = (st * e_mid[:, sl]).astype(BF16)
            o_scr[pl.ds(r0, CHUNK), sl] = (
                jnp.dot(a, v[:, sl], preferred_element_type=F32)
                + lax.dot_general(qd[:, sl], st_mid, (((1,), (1,)), ((), ())),
                                  preferred_element_type=F32))
            st_ref[h] = (st * (e_mid[:, sl] * e_tail[:, sl])
                         + jnp.dot(vt[sl, :], kdec[:, sl], preferred_element_type=F32))
        return carry

    lax.fori_loop(0, ts // CHUNK, block, 0, unroll=True)


def _hgrn_kernel(q_ref, f_ref, i_ref, g_ref, lb_ref, gain_ref, tri_ref, o_ref,
                 st_ref, lf_scr, k_scr, b_scr, bl_scr, kc_scr, vc_scr, o_scr, *, ts):
    hd = HGRN_HEAD_DIM

    @pl.when(pl.program_id(1) == 0)
    def _():
        st_ref[...] = jnp.zeros_like(st_ref)

    lb = lb_ref[...]
    f = lb + (1.0 - lb) * jax.nn.sigmoid(f_ref[...].astype(F32))
    lf = jnp.log(f)
    lf_scr[...] = lf
    k_scr[...] = 1.0 - f
    hi = lf.astype(BF16)
    lo = (lf - hi.astype(F32)).astype(BF16)
    b = (jnp.dot(tri_ref[...], hi, preferred_element_type=F32)
         + jnp.dot(tri_ref[...], lo, preferred_element_type=F32))
    b_scr[...] = b
    mild = jnp.min(b) >= -FAST_MAX_DECAY

    @pl.when(mild)
    def _():
        _hgrn_fast_tile(q_ref, i_ref, st_ref, k_scr, b_scr, o_scr, ts)

    @pl.when(jnp.logical_not(mild))
    def _():
        _hgrn_exact_tile(q_ref, i_ref, st_ref, lf_scr, k_scr, bl_scr, kc_scr, vc_scr, o_scr, ts)

    gate = jax.nn.silu(g_ref[...].astype(F32)) * gain_ref[...]
    for h in range(HGRN_HEADS):
        sl = slice(h * hd, (h + 1) * hd)
        o = o_scr[:, sl]
        o = o * lax.rsqrt(jnp.mean(o * o, axis=-1, keepdims=True) + EPS)
        o_ref[:, sl] = (o * gate[:, sl]).astype(o_ref.dtype)


def _hgrn(proj, lb, gain, *, batch, seq, ts):
    n = batch * seq
    w = HGRN_WIDTH
    steps = seq // ts
    rows = np.arange(ts)
    tri = jnp.asarray((rows[:, None] // FAST == rows[None, :] // FAST) & (rows[:, None] >= rows[None, :]), BF16)

    def col(j):
        return pl.BlockSpec((ts, w), lambda b, i, j=j: (b * steps + i, j))

    vec = pl.BlockSpec((1, w), lambda b, i: (0, 0))
    return pl.pallas_call(
        functools.partial(_hgrn_kernel, ts=ts),
        grid=(batch, steps),
        in_specs=[col(0), col(1), col(2), col(3), vec, vec, pl.BlockSpec((ts, ts), lambda b, i: (0, 0))],
        out_specs=pl.BlockSpec((ts, w), lambda b, i: (b * steps + i, 0)),
        out_shape=jax.ShapeDtypeStruct((n, w), BF16),
        scratch_shapes=[
            pltpu.VMEM((HGRN_HEADS, HGRN_HEAD_DIM, HGRN_HEAD_DIM), F32),
            pltpu.VMEM((ts, w), F32),
            pltpu.VMEM((ts, w), F32),
            pltpu.VMEM((ts, w), F32),
            pltpu.VMEM((SUB, w), F32),
            pltpu.VMEM((SUB, w), F32),
            pltpu.VMEM((SUB, w), F32),
            pltpu.VMEM((ts, w), F32),
        ],
        compiler_params=_params(("parallel", "arbitrary")),
        name="hgrn",
    )(proj, proj, proj, proj, lb, gain, tri)


def _attn_block(q, kwin, vwin, bias_ref, lo, gain, o_ref):
    qs = (q.astype(F32) * (ATTN_HEAD_DIM ** -0.5 * LOG2E)).astype(BF16)
    lane = lax.broadcasted_iota(jnp.int32, (ATTN_QBLK, 128), 1)
    first = lane < ATTN_HEAD_DIM
    ones = jnp.ones((kwin.shape[0], 128), BF16)
    outs = []
    for p in range(ATTN_HEADS // 2):
        sl = slice(p * 128, (p + 1) * 128)
        qp, kp, vp = qs[:, sl], kwin[:, sl], vwin[:, sl]
        zero = jnp.zeros_like(qp)
        q2 = jnp.concatenate([jnp.where(first, qp, zero), jnp.where(first, zero, qp)], axis=0)
        s = lax.dot_general(q2, kp, (((1,), (1,)), ((), ())), preferred_element_type=F32)
        s = s + bias_ref[p, :, lo:]
        e = jnp.exp2(s - jnp.max(s, axis=-1, keepdims=True)).astype(BF16)
        oa = jnp.dot(e, jnp.concatenate([vp, ones], axis=1), preferred_element_type=F32)
        o2 = oa[:, :128] / oa[:, 128:]
        outs.append(jnp.where(first, o2[:ATTN_QBLK], o2[ATTN_QBLK:]))
    o = jnp.concatenate(outs, axis=1)
    o_ref[...] = _rms(o, gain).astype(o_ref.dtype)


def _attn_kernel(q_ref, k_ref, v_ref, bias_ref, gain_ref, o_ref):
    i = pl.program_id(1)
    gain = gain_ref[...]
    left = LEFT_CHUNKS * CHUNK

    for blk in range(left // ATTN_QBLK):
        @pl.when(i == blk)
        def _(blk=blk):
            w = (blk + 1) * ATTN_QBLK
            _attn_block(q_ref[...], k_ref[0:w, :], v_ref[0:w, :], bias_ref, ATTN_WIN - w, gain, o_ref)

    @pl.when(i >= left // ATTN_QBLK)
    def _():
        start = pl.multiple_of(i * ATTN_QBLK - left, ATTN_QBLK)
        _attn_block(q_ref[...], k_ref[pl.ds(start, ATTN_WIN), :], v_ref[pl.ds(start, ATTN_WIN), :],
                    bias_ref, 0, gain, o_ref)


def _attn_bias(rel_bias):
    q, w = ATTN_QBLK, ATTN_WIN
    span = q + w - 1
    r = LEFT_CHUNKS * CHUNK + (q - 1) - np.arange(span)
    vec = rel_bias.astype(F32)[:, np.clip(r, -MAX_REL, MAX_REL) + MAX_REL]
    skew = jnp.tile(vec, (1, q + 1))[:, :q * (span + 1)].reshape(-1, q, span + 1)
    bias = skew[:, ::-1, :w]
    qi = np.arange(q)[:, None]
    kj = np.arange(w)[None, :]
    dchunk = qi // CHUNK + LEFT_CHUNKS - kj // CHUNK
    in_band = (dchunk >= 0) & (dchunk <= LEFT_CHUNKS)
    bias = jnp.where(in_band[None], bias * LOG2E, NEG)
    return bias.reshape(-1, 2 * q, w)


def _attn(proj, bias, gain, *, batch, seq):
    n = batch * seq
    w = ATTN_WIDTH
    steps = seq // ATTN_QBLK
    qcol = 4 * HGRN_WIDTH // w
    return pl.pallas_call(
        _attn_kernel,
        grid=(batch, steps),
        in_specs=[
            pl.BlockSpec((ATTN_QBLK, w), lambda b, i: (b * steps + i, qcol)),
            pl.BlockSpec((seq, w), lambda b, i: (b, qcol + 1)),
            pl.BlockSpec((seq, w), lambda b, i: (b, qcol + 2)),
            pl.BlockSpec((ATTN_HEADS // 2, 2 * ATTN_QBLK, ATTN_WIN), lambda b, i: (0, 0, 0)),
            pl.BlockSpec((1, w), lambda b, i: (0, 0)),
        ],
        out_specs=pl.BlockSpec((ATTN_QBLK, w), lambda b, i: (b * steps + i, 0)),
        out_shape=jax.ShapeDtypeStruct((n, w), BF16),
        compiler_params=_params(("parallel", "arbitrary")),
        name="attn",
    )(proj, proj, proj, bias, gain)


def _post_kernel(h_ref, yh_ref, ya_ref, wout_ref, gq_ref, wq_ref, kv_ref, wo_ref, o_ref):
    d = h_ref.shape[1]
    hw = yh_ref.shape[1]
    h1 = (h_ref[...]
          + jnp.dot(yh_ref[...], wout_ref[0:hw, :], preferred_element_type=F32)
          + jnp.dot(ya_ref[...], wout_ref[hw:, :], preferred_element_type=F32))
    hn = _rms(h1, gq_ref[...]).astype(BF16)
    dh = d // MEM_HEADS
    q = (jnp.dot(hn, wq_ref[...], preferred_element_type=F32) * (dh ** -0.5)).astype(BF16)
    outs = []
    for hd in range(MEM_HEADS):
        sl = slice(hd * dh, (hd + 1) * dh)
        k = kv_ref[:, sl]
        v = kv_ref[:, d + hd * dh:d + (hd + 1) * dh]
        s = lax.dot_general(q[:, sl], k, (((1,), (1,)), ((), ())), preferred_element_type=F32)
        e, l = _softmax_rows(s)
        outs.append((jnp.dot(e.astype(BF16), v, preferred_element_type=F32) / l).astype(BF16))
    o = jnp.concatenate(outs, axis=1)
    o_ref[...] = h1 + jnp.dot(o, wo_ref[...], preferred_element_type=F32)


def _post(h, yh, ya, w_out, gq, wq, kv, wo, *, seq, mem_len, tm):
    n, d = h.shape
    per_batch = seq // tm
    const = lambda i: (0, 0)
    return pl.pallas_call(
        _post_kernel,
        grid=(n // tm,),
        in_specs=[
            pl.BlockSpec((tm, d), lambda i: (i, 0)),
            pl.BlockSpec((tm, yh.shape[1]), lambda i: (i, 0)),
            pl.BlockSpec((tm, ya.shape[1]), lambda i: (i, 0)),
            pl.BlockSpec(w_out.shape, const),
            pl.BlockSpec((1, d), const),
            pl.BlockSpec(wq.shape, const),
            pl.BlockSpec((mem_len, 2 * d), lambda i: (i // per_batch, 0)),
            pl.BlockSpec(wo.shape, const),
        ],
        out_specs=pl.BlockSpec((tm, d), lambda i: (i, 0)),
        out_shape=jax.ShapeDtypeStruct((n, d), F32),
        compiler_params=_params(("parallel",)),
        name="post",
    )(h, yh, ya, w_out, gq, wq, kv, wo)


def _ffn_kernel(h_ref, g_ref, wg_ref, wu_ref, wd_ref, o_ref, xn_scr, acc_scr):
    f = pl.program_id(1)

    @pl.when(f == 0)
    def _():
        xn_scr[...] = _rms(h_ref[...], g_ref[...]).astype(BF16)
        acc_scr[...] = jnp.zeros_like(acc_scr)

    xn = xn_scr[...]
    a = jnp.dot(xn, wg_ref[...], preferred_element_type=F32)
    u = jnp.dot(xn, wu_ref[...], preferred_element_type=F32)
    act = (jax.nn.silu(a) * u).astype(BF16)
    acc_scr[...] += jnp.dot(act, wd_ref[...], preferred_element_type=F32)

    @pl.when(f == pl.num_programs(1) - 1)
    def _():
        o_ref[...] = h_ref[...] + acc_scr[...]


def _ffn(h, gain, wg, wu, wd, *, tm, tf):
    n, d = h.shape
    dff = wg.shape[1]
    return pl.pallas_call(
        _ffn_kernel,
        grid=(n // tm, dff // tf),
        in_specs=[
            pl.BlockSpec((tm, d), lambda i, f: (i, 0)),
            pl.BlockSpec((1, d), lambda i, f: (0, 0)),
            pl.BlockSpec((d, tf), lambda i, f: (0, f)),
            pl.BlockSpec((d, tf), lambda i, f: (0, f)),
            pl.BlockSpec((tf, d), lambda i, f: (f, 0)),
        ],
        out_specs=pl.BlockSpec((tm, d), lambda i, f: (i, 0)),
        out_shape=jax.ShapeDtypeStruct((n, d), F32),
        scratch_shapes=[pltpu.VMEM((tm, d), BF16), pltpu.VMEM((tm, d), F32)],
        compiler_params=_params(("parallel", "arbitrary")),
        name="ffn",
    )(h, gain, wg, wu, wd)


def _router_kernel(h_ref, g_ref, w1_ref, w2_ref, tri_ref, ri_ref, rg_ref, cnt_ref, hn_ref, carry_scr):
    i = pl.program_id(0)
    tm = h_ref.shape[0]

    @pl.when(i == 0)
    def _():
        carry_scr[...] = jnp.zeros_like(carry_scr)

    hn = _rms(h_ref[...], g_ref[...])
    for c in range(LANE_TILES):
        hn_ref[pl.ds(c, tm, stride=LANE_TILES), :] = hn[:, c * 128:(c + 1) * 128]
    a1 = hn.astype(BF16)
    a2 = (hn - a1.astype(F32)).astype(BF16)
    lg = (jnp.dot(a1, w1_ref[...], preferred_element_type=F32)
          + jnp.dot(a1, w2_ref[...], preferred_element_type=F32)
          + jnp.dot(a2, w1_ref[...], preferred_element_type=F32))
    lgt = lg.T[0:N_EXPERTS, :]
    ids = lax.broadcasted_iota(jnp.int32, (N_EXPERTS, tm), 0)
    m1 = jnp.max(lgt, axis=0, keepdims=True)
    i1 = jnp.min(jnp.where(lgt == m1, ids, N_EXPERTS), axis=0, keepdims=True)
    rest = jnp.where(ids == i1, -jnp.inf, lgt)
    m2 = jnp.max(rest, axis=0, keepdims=True)
    i2 = jnp.min(jnp.where(rest == m2, ids, N_EXPERTS), axis=0, keepdims=True)
    e = jnp.exp(m2 - m1)
    g1 = 1.0 / (1.0 + e)
    g2 = e * g1
    hit1 = ids == i1
    hit2 = ids == i2
    onehot = jnp.where(hit1 | hit2, 1.0, 0.0)
    before = jnp.dot(onehot.astype(BF16), tri_ref[...], preferred_element_type=F32) + carry_scr[...][:, 0:1]
    r1 = jnp.sum(jnp.where(hit1, before, 0.0), axis=0, keepdims=True)
    r2 = jnp.sum(jnp.where(hit2, before, 0.0), axis=0, keepdims=True)
    carry_scr[...] = carry_scr[...] + jnp.sum(onehot, axis=1, keepdims=True)
    cnt_ref[...] = carry_scr[...]
    row = lax.broadcasted_iota(jnp.int32, (8, tm), 0)
    ri_ref[...] = jnp.where(row == 0, i1, jnp.where(row == 1, i2, jnp.where(
        row == 2, r1.astype(jnp.int32), jnp.where(row == 3, r2.astype(jnp.int32), 0))))
    rg_ref[...] = jnp.where(row == 0, g1, jnp.where(row == 1, g2, 0.0))


def _router(h, gain, w1, w2, tri, *, tm):
    n, d = h.shape
    return pl.pallas_call(
        _router_kernel,
        grid=(n // tm,),
        in_specs=[
            pl.BlockSpec((tm, d), lambda i: (i, 0)),
            pl.BlockSpec((1, d), lambda i: (0, 0)),
            pl.BlockSpec(w1.shape, lambda i: (0, 0)),
            pl.BlockSpec(w2.shape, lambda i: (0, 0)),
            pl.BlockSpec((tm, tm), lambda i: (0, 0)),
        ],
        out_specs=[
            pl.BlockSpec((8, tm), lambda i: (0, i)),
            pl.BlockSpec((8, tm), lambda i: (0, i)),
            pl.BlockSpec((N_EXPERTS, 128), lambda i: (0, 0)),
            pl.BlockSpec((tm * LANE_TILES, 128), lambda i: (i, 0)),
        ],
        out_shape=[
            jax.ShapeDtypeStruct((8, n), jnp.int32),
            jax.ShapeDtypeStruct((8, n), F32),
            jax.ShapeDtypeStruct((N_EXPERTS, 128), F32),
            jax.ShapeDtypeStruct((n * LANE_TILES, 128), F32),
        ],
        scratch_shapes=[pltpu.VMEM((N_EXPERTS, 128), F32)],
        compiler_params=_params(("arbitrary",)),
        name="router",
    )(h, gain, w1, w2, tri)


def _experts_kernel(be_ref, nu_ref, idx_hbm, h_hbm, wg_ref, wu_ref, wd_ref, y_hbm,
                    idx_smem, xbuf, obuf, xb_scr, acc_scr, isem, gsem, ssem, *, n_tokens):
    j = pl.program_id(0)
    f = pl.program_id(1)
    nf = pl.num_programs(1)
    nu = nu_ref[0]
    rows_per_step = MOE_ROWS // MOE_STEPS

    def idx_copy(b):
        s = lax.rem(b + 1, 4)
        return pltpu.make_async_copy(idx_hbm.at[b + 1], idx_smem.at[s], isem.at[s])

    def gather(b, step, u):
        tok = idx_smem[lax.rem(b + 1, 4), step, u]
        s = lax.rem(b + 2, 2)
        r = step * rows_per_step + u
        return pltpu.make_async_copy(h_hbm.at[pl.ds(tok * LANE_TILES, LANE_TILES)],
                                     xbuf.at[s, pl.ds(r * LANE_TILES, LANE_TILES)], gsem.at[s])

    def scatter(b, step, u):
        tgt = idx_smem[lax.rem(b + 1, 4), MOE_STEPS + step, u]
        s = lax.rem(b + 2, 2)
        r = step * rows_per_step + u
        return pltpu.make_async_copy(obuf.at[s, pl.ds(r * LANE_TILES, LANE_TILES)],
                                     y_hbm.at[pl.ds(tgt * LANE_TILES, LANE_TILES)], ssem.at[s])

    def wait_gathers(b):
        s = lax.rem(b + 2, 2)
        pltpu.make_async_copy(h_hbm.at[pl.ds(0, MOE_ROWS * LANE_TILES)], xbuf.at[s], gsem.at[s]).wait()

    def wait_scatters(b):
        s = lax.rem(b + 2, 2)
        pltpu.make_async_copy(obuf.at[s], y_hbm.at[pl.ds(0, MOE_ROWS * LANE_TILES)], ssem.at[s]).wait()

    def for_rows(fn, steps=range(MOE_STEPS)):
        for step in steps:
            def body(u, carry, step=step):
                fn(step, u)
                return carry
            lax.fori_loop(0, rows_per_step, body, 0, unroll=16)

    @pl.when((j == 0) & (f == 0))
    def _():
        obuf[1] = jnp.zeros(obuf.shape[1:], obuf.dtype)
        for half in range(2):
            spare = pltpu.make_async_copy(
                obuf.at[1],
                y_hbm.at[pl.ds((TOP_K * n_tokens + half * MOE_ROWS) * LANE_TILES, MOE_ROWS * LANE_TILES)],
                ssem.at[half])
            spare.start()
            spare.wait()
        for b in (-1, 0, 1):
            idx_copy(b).start()
        idx_copy(-1).wait()
        idx_copy(0).wait()
        for_rows(lambda step, u: gather(0, step, u).start())

    @pl.when((f == 0) & (j <= nu))
    def _():
        wait_gathers(j)

        @pl.when(j >= 1)
        def _():
            wait_scatters(j - 2)

        idx_copy(j + 1).wait()

        @pl.when(j < nu)
        def _():
            idx_copy(j + 2).start()

    @pl.when(j < nu)
    def _():
        slot = lax.rem(j, 2)

        @pl.when(f == 0)
        def _():
            for c in range(LANE_TILES):
                xb_scr[:, c * 128:(c + 1) * 128] = (
                    xbuf[slot, pl.ds(c, MOE_ROWS, stride=LANE_TILES), :].astype(BF16))
            acc_scr[...] = jnp.zeros_like(acc_scr)

        for u in range(rows_per_step):
            gather(j + 1, f, u).start()
            scatter(j - 1, f, u).start()
        xb = xb_scr[...]
        a = jnp.dot(xb, wg_ref[...], preferred_element_type=F32)
        u_ = jnp.dot(xb, wu_ref[...], preferred_element_type=F32)
        act = (jax.nn.silu(a) * u_).astype(BF16)
        acc_scr[...] += jnp.dot(act, wd_ref[...], preferred_element_type=F32)

        @pl.when(f == nf - 1)
        def _():
            for c in range(LANE_TILES):
                obuf[slot, pl.ds(c, MOE_ROWS, stride=LANE_TILES), :] = acc_scr[:, c * 128:(c + 1) * 128]

    @pl.when(j == nu)
    def _():
        for_rows(lambda step, u: scatter(j - 1, step, u).start(), steps=[f])

        @pl.when(f == nf - 1)
        def _():
            wait_scatters(j - 1)


def _experts(block_expert, n_used, idx, h3, wg, wu, wd):
    n = h3.shape[0] // LANE_TILES
    d = LANE_TILES * 128
    dff = wg.shape[2]
    tf = dff // MOE_STEPS
    n_blocks = idx.shape[0] - 3

    def fcol(j, f, nu):
        return jnp.where(j < nu[0], f, MOE_STEPS - 1)

    def eblk(j, be, nu):
        return be[jnp.minimum(j, nu[0] - 1)]

    grid_spec = pltpu.PrefetchScalarGridSpec(
        num_scalar_prefetch=2,
        grid=(n_blocks + 1, MOE_STEPS),
        in_specs=[
            pl.BlockSpec(memory_space=pl.ANY),
            pl.BlockSpec(memory_space=pl.ANY),
            pl.BlockSpec((None, d, tf), lambda j, f, be, nu: (eblk(j, be, nu), 0, fcol(j, f, nu))),
            pl.BlockSpec((None, d, tf), lambda j, f, be, nu: (eblk(j, be, nu), 0, fcol(j, f, nu))),
            pl.BlockSpec((None, tf, d), lambda j, f, be, nu: (eblk(j, be, nu), fcol(j, f, nu), 0)),
        ],
        out_specs=pl.BlockSpec(memory_space=pl.ANY),
        scratch_shapes=[
            pltpu.SMEM((4, 2 * MOE_STEPS, MOE_ROWS // MOE_STEPS), jnp.int32),
            pltpu.VMEM((2, MOE_ROWS * LANE_TILES, 128), F32),
            pltpu.VMEM((2, MOE_ROWS * LANE_TILES, 128), F32),
            pltpu.VMEM((MOE_ROWS, d), BF16),
            pltpu.VMEM((MOE_ROWS, d), F32),
            pltpu.SemaphoreType.DMA((4,)),
            pltpu.SemaphoreType.DMA((2,)),
            pltpu.SemaphoreType.DMA((2,)),
        ],
    )
    return pl.pallas_call(
        functools.partial(_experts_kernel, n_tokens=n),
        grid_spec=grid_spec,
        out_shape=jax.ShapeDtypeStruct(((TOP_K * n + 2 * MOE_ROWS) * LANE_TILES, 128), F32),
        compiler_params=_params(("arbitrary", "arbitrary")),
        name="experts",
    )(block_expert, n_used, idx, h3, wg, wu, wd)


def _combine_kernel(h_ref, rg_ref, gf_ref, y0_ref, y1_ref, o_ref, *, final_norm):
    gates = rg_ref[...].T
    g0, g1 = gates[:, 0:1], gates[:, 1:2]
    tm = h_ref.shape[0]
    out = jnp.concatenate([g0 * y0_ref[pl.ds(c, tm, stride=LANE_TILES), :]
                           + g1 * y1_ref[pl.ds(c, tm, stride=LANE_TILES), :]
                           for c in range(LANE_TILES)], axis=1)
    out = h_ref[...] + out
    if final_norm:
        out = _rms(out, gf_ref[...])
    o_ref[...] = out


def _combine(h, rg, gain_final, y, *, tm, final_norm):
    n, d = h.shape
    return pl.pallas_call(
        functools.partial(_combine_kernel, final_norm=final_norm),
        grid=(n // tm,),
        in_specs=[
            pl.BlockSpec((tm, d), lambda i: (i, 0)),
            pl.BlockSpec((8, tm), lambda i: (0, i)),
            pl.BlockSpec((1, d), lambda i: (0, 0)),
            pl.BlockSpec((tm * LANE_TILES, 128), lambda i: (i, 0)),
            pl.BlockSpec((tm * LANE_TILES, 128), lambda i: (i + n // tm, 0)),
        ],
        out_specs=pl.BlockSpec((tm, d), lambda i: (i, 0)),
        out_shape=jax.ShapeDtypeStruct((n, d), F32),
        compiler_params=_params(("parallel",)),
        name="combine",
    )(h, rg, gain_final, y, y)


def _moe(h, gain, w_router, wg, wu, wd, gain_final, *, final_norm, tm_route, tm_rows):
    n, d = h.shape
    wr = jnp.zeros((d, 128), F32).at[:, :N_EXPERTS].set(w_router.astype(F32))
    w1 = wr.astype(BF16)
    w2 = (wr - w1.astype(F32)).astype(BF16)
    tri = jnp.asarray(np.triu(np.ones((tm_route, tm_route), np.float32), 1), BF16)
    ri, rg, cnt, hn3 = _router(h, gain, w1, w2, tri, tm=tm_route)

    counts = cnt[:, 0].astype(jnp.int32)
    padded = (counts + MOE_ROWS - 1) // MOE_ROWS * MOE_ROWS
    padded_end = jnp.cumsum(padded)
    padded_start = padded_end - padded
    n_blocks = -(-(n * TOP_K + N_EXPERTS * (MOE_ROWS - 1)) // MOE_ROWS)
    block_start = jnp.arange(n_blocks, dtype=jnp.int32) * MOE_ROWS
    block_expert = jnp.minimum(
        jnp.sum(block_start[:, None] >= padded_end[None, :], axis=1), N_EXPERTS - 1).astype(jnp.int32)
    n_used = (padded_end[-1:] // MOE_ROWS).astype(jnp.int32)
    onehot = ri[0:TOP_K, :, None] == jnp.arange(N_EXPERTS, dtype=jnp.int32)
    dest = jnp.sum(jnp.where(onehot, padded_start, 0), axis=-1) + ri[TOP_K:2 * TOP_K]

    n_idx = (n_blocks + 3) * MOE_ROWS
    pos = jnp.arange(n_idx, dtype=jnp.int32)
    spare = TOP_K * n + (pos // MOE_ROWS % 2) * MOE_ROWS + pos % MOE_ROWS
    target = spare.at[dest.reshape(-1) + MOE_ROWS].set(
        jnp.arange(TOP_K * n, dtype=jnp.int32), unique_indices=True)
    token = jnp.where(target < TOP_K * n, target % n, 0)
    per_step = MOE_ROWS // MOE_STEPS
    idx = jnp.concatenate([token.reshape(-1, MOE_STEPS, per_step),
                           target.reshape(-1, MOE_STEPS, per_step)], axis=1)

    y = _experts(block_expert, n_used, idx, hn3, wg, wu, wd)
    return _combine(h, rg, gain_final, y, tm=tm_rows, final_norm=final_norm)


def kernel(x, mem, w_in, hgrn_lower_bound, hgrn_out_gain, attn_rel_bias, attn_out_gain, w_out,
           norm_mix, norm_mem_q, norm_mem_kv, w_mem_q, w_mem_kv, w_mem_o, norm_ffn,
           w_ffn_gate, w_ffn_up, w_ffn_down, w_router, w_exp_gate, w_exp_up, w_exp_down, norm_final):
    batch, seq, d = x.shape
    mem_len = mem.shape[1]
    depth = w_in.shape[0]
    n = batch * seq
    assert seq % 512 == 0 and seq >= ATTN_WIN and n % 1024 == 0

    lb_probs = jax.nn.softmax(hgrn_lower_bound.astype(F32), axis=0)
    lower_bounds = jnp.cumsum(lb_probs, axis=0) - lb_probs[0]

    h = x.reshape(n, d)
    mem2 = mem.reshape(batch * mem_len, d)
    row = lambda v: v.reshape(1, -1).astype(F32)
    for layer in range(depth):
        proj = _norm_matmul(h, row(norm_mix[layer]), w_in[layer].astype(BF16),
                            tm=512, col_chunk=512, name="in_proj")
        y_hgrn = _hgrn(proj, row(lower_bounds[layer]), row(hgrn_out_gain[layer]),
                       batch=batch, seq=seq, ts=512)
        y_attn = _attn(proj, _attn_bias(attn_rel_bias[layer]), row(attn_out_gain[layer]),
                       batch=batch, seq=seq)
        kv = _norm_matmul(mem2, row(norm_mem_kv[layer]), w_mem_kv[layer].astype(BF16),
                          tm=min(512, batch * mem_len), col_chunk=512, name="mem_kv")
        h = _post(h, y_hgrn, y_attn, w_out[layer].astype(BF16), row(norm_mem_q[layer]),
                  w_mem_q[layer].astype(BF16), kv, w_mem_o[layer].astype(BF16),
                  seq=seq, mem_len=mem_len, tm=512)
        j = layer // 2
        if layer % 2 == 0:
            h = _ffn(h, row(norm_ffn[layer]), w_ffn_gate[j].astype(BF16), w_ffn_up[j].astype(BF16),
                     w_ffn_down[j].astype(BF16), tm=1024, tf=256)
        else:
            last = layer == depth - 1
            h = _moe(h, row(norm_ffn[layer]), w_router[j], w_exp_gate[j].astype(BF16),
                     w_exp_up[j].astype(BF16), w_exp_down[j].astype(BF16), row(norm_final),
                     final_norm=last, tm_route=512, tm_rows=256)
    if depth % 2 == 1:
        h = _norm_only(h, row(norm_final))
    return h.reshape(batch, seq, d)


def _norm_only_kernel(x_ref, g_ref, o_ref):
    o_ref[...] = _rms(x_ref[...], g_ref[...])


def _norm_only(x, gain, tm=1024):
    n, d = x.shape
    return pl.pallas_call(
        _norm_only_kernel,
        grid=(n // tm,),
        in_specs=[pl.BlockSpec((tm, d), lambda i: (i, 0)), pl.BlockSpec((1, d), lambda i: (0, 0))],
        out_specs=pl.BlockSpec((tm, d), lambda i: (i, 0)),
        out_shape=jax.ShapeDtypeStruct((n, d), F32),
        compiler_params=_params(("parallel",)),
        name="final_norm",
    )(x, gain)
```

```python
import functools

import numpy as np
import jax
import jax.numpy as jnp
from jax import lax
from jax.experimental import pallas as pl
from jax.experimental.pallas import tpu as pltpu

F32 = jnp.float32
BF16 = jnp.bfloat16

EPS = 1e-6
CHUNK = 64
LEFT_CHUNKS = 8
MAX_REL = 128
HGRN_HEADS = 4
HGRN_HEAD_DIM = 128
HGRN_WIDTH = HGRN_HEADS * HGRN_HEAD_DIM
ATTN_HEADS = 8
ATTN_HEAD_DIM = 64
ATTN_WIDTH = ATTN_HEADS * ATTN_HEAD_DIM
MEM_HEADS = 4
N_EXPERTS = 8
TOP_K = 2

SUB = 16
FAST = 32
FAST_MAX_DECAY = 60.0
ATTN_QBLK = 4 * CHUNK
ATTN_WIN = ATTN_QBLK + LEFT_CHUNKS * CHUNK
MOE_STEPS = 7
LANE_TILES = 8
MOE_ROWS = 144 * MOE_STEPS
NEG = -1e30
LOG2E = 1.4426950408889634

VMEM_LIMIT = 56 * 1024 * 1024


def _params(sem):
    return pltpu.CompilerParams(dimension_semantics=sem, vmem_limit_bytes=VMEM_LIMIT)


def _rms(x, gain):
    ms = jnp.mean(x * x, axis=-1, keepdims=True)
    return x * lax.rsqrt(ms + EPS) * gain


def _softmax_rows(s):
    m = jnp.max(s, axis=-1, keepdims=True)
    e = jnp.exp(s - m)
    return e, jnp.sum(e, axis=-1, keepdims=True)


def _norm_matmul_kernel(x_ref, g_ref, w_ref, o_ref, *, col_chunk):
    xn = _rms(x_ref[...], g_ref[...]).astype(BF16)
    n_out = o_ref.shape[1]
    for c in range(n_out // col_chunk):
        sl = slice(c * col_chunk, (c + 1) * col_chunk)
        o_ref[:, sl] = jnp.dot(xn, w_ref[:, sl], preferred_element_type=F32).astype(o_ref.dtype)


def _norm_matmul(x, gain, w, *, tm, col_chunk, name):
    n, d = x.shape
    n_out = w.shape[1]
    return pl.pallas_call(
        functools.partial(_norm_matmul_kernel, col_chunk=col_chunk),
        grid=(n // tm,),
        in_specs=[
            pl.BlockSpec((tm, d), lambda i: (i, 0)),
            pl.BlockSpec((1, d), lambda i: (0, 0)),
            pl.BlockSpec((d, n_out), lambda i: (0, 0)),
        ],
        out_specs=pl.BlockSpec((tm, n_out), lambda i: (i, 0)),
        out_shape=jax.ShapeDtypeStruct((n, n_out), BF16),
        compiler_params=_params(("parallel",)),
        name=name,
    )(x, gain, w)


def _hgrn_exact_tile(q_ref, i_ref, st_ref, lf_scr, k_scr, bl_scr, kc_scr, vc_scr, o_scr, ts):
    hd = HGRN_HEAD_DIM
    r = lax.broadcasted_iota(jnp.int32, (SUB, SUB), 0)
    c = lax.broadcasted_iota(jnp.int32, (SUB, SUB), 1)
    ltri = jnp.where(r >= c, 1.0, 0.0).astype(BF16)
    rowid = lax.broadcasted_iota(jnp.int32, (SUB, hd), 0)

    def sub_chunk(j, carry):
        r0 = pl.multiple_of(j * SUB, SUB)
        lf = lf_scr[pl.ds(r0, SUB), :]
        hi = lf.astype(BF16)
        lo = (lf - hi.astype(F32)).astype(BF16)
        bl = (jnp.dot(ltri, hi, preferred_element_type=F32)
              + jnp.dot(ltri, lo, preferred_element_type=F32))
        bl_scr[...] = bl
        kc_scr[...] = k_scr[pl.ds(r0, SUB), :]
        vc_scr[...] = i_ref[pl.ds(r0, SUB), :].astype(F32)
        qf = q_ref[pl.ds(r0, SUB), :].astype(F32)
        for h in range(HGRN_HEADS):
            sl = slice(h * hd, (h + 1) * hd)
            blh = bl[:, sl]
            qh = qf[:, sl]
            st = st_ref[h]
            o = lax.dot_general((qh * jnp.exp(blh)).astype(BF16), st.astype(BF16),
                                (((1,), (1,)), ((), ())), preferred_element_type=F32)
            for s in range(SUB):
                bs = bl_scr[s:s + 1, sl]
                ks = kc_scr[s:s + 1, sl]
                vs = vc_scr[s:s + 1, sl]
                e = jnp.exp(jnp.minimum(blh - bs, 0.0))
                a = jnp.sum(qh * e * ks, axis=-1, keepdims=True)
                a = jnp.where(rowid >= s, a, 0.0)
                o = o + a * vs
            o_scr[pl.ds(r0, SUB), sl] = o
            bend = bl_scr[SUB - 1:SUB, sl]
            kdec = (kc_scr[:, sl] * jnp.exp(bend - blh)).astype(BF16)
            vt = vc_scr[:, sl].T.astype(BF16)
            st_ref[h] = st * jnp.exp(bend) + jnp.dot(vt, kdec, preferred_element_type=F32)
        return carry

    lax.fori_loop(0, ts // SUB, sub_chunk, 0)


def _hgrn_fast_tile(q_ref, i_ref, st_ref, k_scr, b_scr, o_scr, ts):
    hd = HGRN_HEAD_DIM
    r = lax.broadcasted_iota(jnp.int32, (CHUNK, CHUNK), 0)
    c = lax.broadcasted_iota(jnp.int32, (CHUNK, CHUNK), 1)
    causal = r >= c
    first_half = lax.broadcasted_iota(jnp.int32, (CHUNK, HGRN_WIDTH), 0) < FAST

    def block(j, carry):
        r0 = pl.multiple_of(j * CHUNK, CHUNK)
        b = b_scr[pl.ds(r0, CHUNK), :]
        b_mid = b[FAST - 1:FAST, :]
        rel = b - jnp.where(first_half, b_mid, 0.0)
        e_mid = jnp.exp(b_mid)
        e_tail = jnp.exp(b[CHUNK - 1:CHUNK, :])
        kdf = k_scr[pl.ds(r0, CHUNK), :] * jnp.exp(-rel)
        qd = (q_ref[pl.ds(r0, CHUNK), :].astype(F32) * jnp.exp(rel)).astype(BF16)
        kd = kdf.astype(BF16)
        v = i_ref[pl.ds(r0, CHUNK), :]
        kdec = (kdf * e_tail).astype(BF16)
        vt = v.astype(F32).T.astype(BF16)
        for h in range(HGRN_HEADS):
            sl = slice(h * hd, (h + 1) * hd)
            st = st_ref[h]
            a = lax.dot_general(qd[:, sl], kd[:, sl], (((1,), (1,)), ((), ())),
                                preferred_element_type=F32)
            a = jnp.where(causal, a, 0.0).astype(BF16)
            st_mid = (st * e_mid[:, sl]).astype(BF16)
            o_scr[pl.ds(r0, CHUNK), sl] = (
                jnp.dot(a, v[:, sl], preferred_element_type=F32)
                + lax.dot_general(qd[:, sl], st_mid, (((1,), (1,)), ((), ())),
                                  preferred_element_type=F32))
            st_ref[h] = (st * (e_mid[:, sl] * e_tail[:, sl])
                         + jnp.dot(vt[sl, :], kdec[:, sl], preferred_element_type=F32))
        return carry

    lax.fori_loop(0, ts // CHUNK, block, 0, unroll=True)


def _hgrn_kernel(q_ref, f_ref, i_ref, g_ref, lb_ref, gain_ref, tri_ref, o_ref,
                 st_ref, lf_scr, k_scr, b_scr, bl_scr, kc_scr, vc_scr, o_scr, *, ts):
    hd = HGRN_HEAD_DIM

    @pl.when(pl.program_id(1) == 0)
    def _():
        st_ref[...] = jnp.zeros_like(st_ref)

    lb = lb_ref[...]
    f = lb + (1.0 - lb) * jax.nn.sigmoid(f_ref[...].astype(F32))
    lf = jnp.log(f)
    lf_scr[...] = lf
    k_scr[...] = 1.0 - f
    hi = lf.astype(BF16)
    lo = (lf - hi.astype(F32)).astype(BF16)
    nblk = ts // CHUNK
    side = lambda x: jnp.concatenate([x[j * CHUNK:(j + 1) * CHUNK, :] for j in range(nblk)], axis=1)
    b = (jnp.dot(tri_ref[...], side(hi), preferred_element_type=F32)
         + jnp.dot(tri_ref[...], side(lo), preferred_element_type=F32))
    w = lf.shape[1]
    for j in range(nblk):
        b_scr[j * CHUNK:(j + 1) * CHUNK, :] = b[:, j * w:(j + 1) * w]
    mild = jnp.min(b) >= -FAST_MAX_DECAY

    @pl.when(mild)
    def _():
        _hgrn_fast_tile(q_ref, i_ref, st_ref, k_scr, b_scr, o_scr, ts)

    @pl.when(jnp.logical_not(mild))
    def _():
        _hgrn_exact_tile(q_ref, i_ref, st_ref, lf_scr, k_scr, bl_scr, kc_scr, vc_scr, o_scr, ts)

    gate = jax.nn.silu(g_ref[...].astype(F32)) * gain_ref[...]
    for h in range(HGRN_HEADS):
        sl = slice(h * hd, (h + 1) * hd)
        o = o_scr[:, sl]
        o = o * lax.rsqrt(jnp.mean(o * o, axis=-1, keepdims=True) + EPS)
        o_ref[:, sl] = (o * gate[:, sl]).astype(o_ref.dtype)


def _hgrn(proj, lb, gain, *, batch, seq, ts):
    n = batch * seq
    w = HGRN_WIDTH
    steps = seq // ts
    rows = np.arange(CHUNK)
    tri = jnp.asarray((rows[:, None] // FAST == rows[None, :] // FAST) & (rows[:, None] >= rows[None, :]), BF16)

    def col(j):
        return pl.BlockSpec((ts, w), lambda b, i, j=j: (b * steps + i, j))

    vec = pl.BlockSpec((1, w), lambda b, i: (0, 0))
    return pl.pallas_call(
        functools.partial(_hgrn_kernel, ts=ts),
        grid=(batch, steps),
        in_specs=[col(0), col(1), col(2), col(3), vec, vec, pl.BlockSpec((CHUNK, CHUNK), lambda b, i: (0, 0))],
        out_specs=pl.BlockSpec((ts, w), lambda b, i: (b * steps + i, 0)),
        out_shape=jax.ShapeDtypeStruct((n, w), BF16),
        scratch_shapes=[
            pltpu.VMEM((HGRN_HEADS, HGRN_HEAD_DIM, HGRN_HEAD_DIM), F32),
            pltpu.VMEM((ts, w), F32),
            pltpu.VMEM((ts, w), F32),
            pltpu.VMEM((ts, w), F32),
            pltpu.VMEM((SUB, w), F32),
            pltpu.VMEM((SUB, w), F32),
            pltpu.VMEM((SUB, w), F32),
            pltpu.VMEM((ts, w), F32),
        ],
        compiler_params=_params(("parallel", "arbitrary")),
        name="hgrn",
    )(proj, proj, proj, proj, lb, gain, tri)


def _attn_block(q, kwin, vwin, bias_ref, lo, gain, o_ref):
    qs = (q.astype(F32) * (ATTN_HEAD_DIM ** -0.5 * LOG2E)).astype(BF16)
    lane = lax.broadcasted_iota(jnp.int32, (ATTN_QBLK, 128), 1)
    first = lane < ATTN_HEAD_DIM
    ones = jnp.ones((kwin.shape[0], 128), BF16)
    outs = []
    for p in range(ATTN_HEADS // 2):
        sl = slice(p * 128, (p + 1) * 128)
        qp, kp, vp = qs[:, sl], kwin[:, sl], vwin[:, sl]
        zero = jnp.zeros_like(qp)
        q2 = jnp.concatenate([jnp.where(first, qp, zero), jnp.where(first, zero, qp)], axis=0)
        s = lax.dot_general(q2, kp, (((1,), (1,)), ((), ())), preferred_element_type=F32)
        s = s + bias_ref[p, :, lo:]
        e = jnp.exp2(s - jnp.max(s, axis=-1, keepdims=True)).astype(BF16)
        oa = jnp.dot(e, jnp.concatenate([vp, ones], axis=1), preferred_element_type=F32)
        o2 = oa[:, :128] / oa[:, 128:]
        outs.append(jnp.where(first, o2[:ATTN_QBLK], o2[ATTN_QBLK:]))
    o = jnp.concatenate(outs, axis=1)
    o_ref[...] = _rms(o, gain).astype(o_ref.dtype)


def _attn_kernel(q_ref, k_ref, v_ref, bias_ref, gain_ref, o_ref):
    i = pl.program_id(1)
    gain = gain_ref[...]
    left = LEFT_CHUNKS * CHUNK

    for blk in range(left // ATTN_QBLK):
        @pl.when(i == blk)
        def _(blk=blk):
            w = (blk + 1) * ATTN_QBLK
            _attn_block(q_ref[...], k_ref[0:w, :], v_ref[0:w, :], bias_ref, ATTN_WIN - w, gain, o_ref)

    @pl.when(i >= left // ATTN_QBLK)
    def _():
        start = pl.multiple_of(i * ATTN_QBLK - left, ATTN_QBLK)
        _attn_block(q_ref[...], k_ref[pl.ds(start, ATTN_WIN), :], v_ref[pl.ds(start, ATTN_WIN), :],
                    bias_ref, 0, gain, o_ref)


def _attn_bias(rel_bias):
    q, w = ATTN_QBLK, ATTN_WIN
    period = q + w - 1
    x = np.arange(period)
    x = np.where(x < w, x, x - period)
    vec = rel_bias.astype(F32)[:, np.clip(LEFT_CHUNKS * CHUNK - x, -MAX_REL, MAX_REL) + MAX_REL]
    bias = jnp.tile(vec, (1, q))[:, :q * (period - 1)].reshape(-1, q, period - 1)[:, :, :w]
    qi = np.arange(q)[:, None]
    kj = np.arange(w)[None, :]
    dchunk = qi // CHUNK + LEFT_CHUNKS - kj // CHUNK
    in_band = (dchunk >= 0) & (dchunk <= LEFT_CHUNKS)
    bias = jnp.where(in_band[None], bias * LOG2E, NEG)
    return bias.reshape(-1, 2 * q, w)


def _attn(proj, bias, gain, *, batch, seq):
    n = batch * seq
    w = ATTN_WIDTH
    steps = seq // ATTN_QBLK
    qcol = 4 * HGRN_WIDTH // w
    return pl.pallas_call(
        _attn_kernel,
        grid=(batch, steps),
        in_specs=[
            pl.BlockSpec((ATTN_QBLK, w), lambda b, i: (b * steps + i, qcol)),
            pl.BlockSpec((seq, w), lambda b, i: (b, qcol + 1)),
            pl.BlockSpec((seq, w), lambda b, i: (b, qcol + 2)),
            pl.BlockSpec((ATTN_HEADS // 2, 2 * ATTN_QBLK, ATTN_WIN), lambda b, i: (0, 0, 0)),
            pl.BlockSpec((1, w), lambda b, i: (0, 0)),
        ],
        out_specs=pl.BlockSpec((ATTN_QBLK, w), lambda b, i: (b * steps + i, 0)),
        out_shape=jax.ShapeDtypeStruct((n, w), BF16),
        compiler_params=_params(("parallel", "arbitrary")),
        name="attn",
    )(proj, proj, proj, bias, gain)


def _post_kernel(h_ref, yh_ref, ya_ref, wout_ref, gq_ref, wq_ref, kv_ref, wo_ref, o_ref):
    d = h_ref.shape[1]
    hw = yh_ref.shape[1]
    h1 = (h_ref[...]
          + jnp.dot(yh_ref[...], wout_ref[0:hw, :], preferred_element_type=F32)
          + jnp.dot(ya_ref[...], wout_ref[hw:, :], preferred_element_type=F32))
    hn = _rms(h1, gq_ref[...]).astype(BF16)
    dh = d // MEM_HEADS
    q = (jnp.dot(hn, wq_ref[...], preferred_element_type=F32) * (dh ** -0.5)).astype(BF16)
    outs = []
    for hd in range(MEM_HEADS):
        sl = slice(hd * dh, (hd + 1) * dh)
        k = kv_ref[:, sl]
        v = kv_ref[:, d + hd * dh:d + (hd + 1) * dh]
        s = lax.dot_general(q[:, sl], k, (((1,), (1,)), ((), ())), preferred_element_type=F32)
        e, l = _softmax_rows(s)
        outs.append((jnp.dot(e.astype(BF16), v, preferred_element_type=F32) / l).astype(BF16))
    o = jnp.concatenate(outs, axis=1)
    o_ref[...] = h1 + jnp.dot(o, wo_ref[...], preferred_element_type=F32)


def _post(h, yh, ya, w_out, gq, wq, kv, wo, *, seq, mem_len, tm):
    n, d = h.shape
    per_batch = seq // tm
    const = lambda i: (0, 0)
    return pl.pallas_call(
        _post_kernel,
        grid=(n // tm,),
        in_specs=[
            pl.BlockSpec((tm, d), lambda i: (i, 0)),
            pl.BlockSpec((tm, yh.shape[1]), lambda i: (i, 0)),
            pl.BlockSpec((tm, ya.shape[1]), lambda i: (i, 0)),
            pl.BlockSpec(w_out.shape, const),
            pl.BlockSpec((1, d), const),
            pl.BlockSpec(wq.shape, const),
            pl.BlockSpec((mem_len, 2 * d), lambda i: (i // per_batch, 0)),
            pl.BlockSpec(wo.shape, const),
        ],
        out_specs=pl.BlockSpec((tm, d), lambda i: (i, 0)),
        out_shape=jax.ShapeDtypeStruct((n, d), F32),
        compiler_params=_params(("parallel",)),
        name="post",
    )(h, yh, ya, w_out, gq, wq, kv, wo)


def _ffn_kernel(h_ref, g_ref, wg_ref, wu_ref, wd_ref, o_ref, xn_scr, acc_scr, *, tf):
    xn_scr[...] = _rms(h_ref[...], g_ref[...]).astype(BF16)
    acc_scr[...] = jnp.zeros_like(acc_scr)

    def slab(c, carry):
        c0 = pl.multiple_of(c * tf, tf)
        xn = xn_scr[...]
        a = jnp.dot(xn, wg_ref[:, pl.ds(c0, tf)], preferred_element_type=F32)
        u = jnp.dot(xn, wu_ref[:, pl.ds(c0, tf)], preferred_element_type=F32)
        act = (jax.nn.silu(a) * u).astype(BF16)
        acc_scr[...] += jnp.dot(act, wd_ref[pl.ds(c0, tf), :], preferred_element_type=F32)
        return carry

    lax.fori_loop(0, wg_ref.shape[1] // tf, slab, 0)
    o_ref[...] = h_ref[...] + acc_scr[...]


def _ffn(h, gain, wg, wu, wd, *, tm, tf):
    n, d = h.shape
    dff = wg.shape[1]
    resident = dict(pipeline_mode=pl.Buffered(1))
    return pl.pallas_call(
        functools.partial(_ffn_kernel, tf=tf),
        grid=(n // tm,),
        in_specs=[
            pl.BlockSpec((tm, d), lambda i: (i, 0)),
            pl.BlockSpec((1, d), lambda i: (0, 0)),
            pl.BlockSpec((d, dff), lambda i: (0, 0), **resident),
            pl.BlockSpec((d, dff), lambda i: (0, 0), **resident),
            pl.BlockSpec((dff, d), lambda i: (0, 0), **resident),
        ],
        out_specs=pl.BlockSpec((tm, d), lambda i: (i, 0)),
        out_shape=jax.ShapeDtypeStruct((n, d), F32),
        scratch_shapes=[pltpu.VMEM((tm, d), BF16), pltpu.VMEM((tm, d), F32)],
        compiler_params=_params(("parallel",)),
        name="ffn",
    )(h, gain, wg, wu, wd)


def _router_kernel(h_ref, g_ref, w1_ref, w2_ref, tri_ref, ri_ref, rg_ref, cnt_ref, hn_ref, carry_scr):
    i = pl.program_id(0)
    tm = h_ref.shape[0]

    @pl.when(i == 0)
    def _():
        carry_scr[...] = jnp.zeros_like(carry_scr)

    hn = _rms(h_ref[...], g_ref[...])
    for c in range(LANE_TILES):
        hn_ref[pl.ds(c, tm, stride=LANE_TILES), :] = hn[:, c * 128:(c + 1) * 128]
    a1 = hn.astype(BF16)
    a2 = (hn - a1.astype(F32)).astype(BF16)
    lg = (jnp.dot(a1, w1_ref[...], preferred_element_type=F32)
          + jnp.dot(a1, w2_ref[...], preferred_element_type=F32)
          + jnp.dot(a2, w1_ref[...], preferred_element_type=F32))
    lgt = lg.T[0:N_EXPERTS, :]
    ids = lax.broadcasted_iota(jnp.int32, (N_EXPERTS, tm), 0)
    m1 = jnp.max(lgt, axis=0, keepdims=True)
    i1 = jnp.min(jnp.where(lgt == m1, ids, N_EXPERTS), axis=0, keepdims=True)
    rest = jnp.where(ids == i1, -jnp.inf, lgt)
    m2 = jnp.max(rest, axis=0, keepdims=True)
    i2 = jnp.min(jnp.where(rest == m2, ids, N_EXPERTS), axis=0, keepdims=True)
    e = jnp.exp(m2 - m1)
    g1 = 1.0 / (1.0 + e)
    g2 = e * g1
    hit1 = ids == i1
    hit2 = ids == i2
    onehot = jnp.where(hit1 | hit2, 1.0, 0.0)
    before = jnp.dot(onehot.astype(BF16), tri_ref[...], preferred_element_type=F32) + carry_scr[...][:, 0:1]
    r1 = jnp.sum(jnp.where(hit1, before, 0.0), axis=0, keepdims=True)
    r2 = jnp.sum(jnp.where(hit2, before, 0.0), axis=0, keepdims=True)
    carry_scr[...] = carry_scr[...] + jnp.sum(onehot, axis=1, keepdims=True)
    cnt_ref[...] = carry_scr[...]
    row = lax.broadcasted_iota(jnp.int32, (8, tm), 0)
    ri_ref[...] = jnp.where(row == 0, i1, jnp.where(row == 1, i2, jnp.where(
        row == 2, r1.astype(jnp.int32), jnp.where(row == 3, r2.astype(jnp.int32), 0))))
    rg_ref[...] = jnp.where(row == 0, g1, jnp.where(row == 1, g2, 0.0))


def _router(h, gain, w1, w2, tri, *, tm):
    n, d = h.shape
    return pl.pallas_call(
        _router_kernel,
        grid=(n // tm,),
        in_specs=[
            pl.BlockSpec((tm, d), lambda i: (i, 0)),
            pl.BlockSpec((1, d), lambda i: (0, 0)),
            pl.BlockSpec(w1.shape, lambda i: (0, 0)),
            pl.BlockSpec(w2.shape, lambda i: (0, 0)),
            pl.BlockSpec((tm, tm), lambda i: (0, 0)),
        ],
        out_specs=[
            pl.BlockSpec((8, tm), lambda i: (0, i)),
            pl.BlockSpec((8, tm), lambda i: (0, i)),
            pl.BlockSpec((N_EXPERTS, 128), lambda i: (0, 0)),
            pl.BlockSpec((tm * LANE_TILES, 128), lambda i: (i, 0)),
        ],
        out_shape=[
            jax.ShapeDtypeStruct((8, n), jnp.int32),
            jax.ShapeDtypeStruct((8, n), F32),
            jax.ShapeDtypeStruct((N_EXPERTS, 128), F32),
            jax.ShapeDtypeStruct((n * LANE_TILES, 128), F32),
        ],
        scratch_shapes=[pltpu.VMEM((N_EXPERTS, 128), F32)],
        compiler_params=_params(("arbitrary",)),
        name="router",
    )(h, gain, w1, w2, tri)


def _experts_kernel(be_ref, nu_ref, idx_hbm, h_hbm, wg_ref, wu_ref, wd_ref, y_hbm,
                    idx_smem, xbuf, obuf, xb_scr, acc_scr, isem, gsem, ssem, *, n_tokens):
    j = pl.program_id(0)
    f = pl.program_id(1)
    nf = pl.num_programs(1)
    nu = nu_ref[0]
    rows_per_step = MOE_ROWS // MOE_STEPS

    def idx_copy(b):
        s = lax.rem(b + 1, 4)
        return pltpu.make_async_copy(idx_hbm.at[b + 1], idx_smem.at[s], isem.at[s])

    def gather(b, step, u):
        tok = idx_smem[lax.rem(b + 1, 4), step, u]
        s = lax.rem(b + 2, 2)
        r = step * rows_per_step + u
        return pltpu.make_async_copy(h_hbm.at[pl.ds(tok * LANE_TILES, LANE_TILES)],
                                     xbuf.at[s, pl.ds(r * LANE_TILES, LANE_TILES)], gsem.at[s])

    def scatter(b, step, u):
        tgt = idx_smem[lax.rem(b + 1, 4), MOE_STEPS + step, u]
        s = lax.rem(b + 2, 2)
        r = step * rows_per_step + u
        return pltpu.make_async_copy(obuf.at[s, pl.ds(r * LANE_TILES, LANE_TILES)],
                                     y_hbm.at[pl.ds(tgt * LANE_TILES, LANE_TILES)], ssem.at[s])

    def wait_gathers(b):
        s = lax.rem(b + 2, 2)
        pltpu.make_async_copy(h_hbm.at[pl.ds(0, MOE_ROWS * LANE_TILES)], xbuf.at[s], gsem.at[s]).wait()

    def wait_scatters(b):
        s = lax.rem(b + 2, 2)
        pltpu.make_async_copy(obuf.at[s], y_hbm.at[pl.ds(0, MOE_ROWS * LANE_TILES)], ssem.at[s]).wait()

    def for_rows(fn, steps=range(MOE_STEPS)):
        for step in steps:
            def body(u, carry, step=step):
                fn(step, u)
                return carry
            lax.fori_loop(0, rows_per_step, body, 0, unroll=16)

    @pl.when((j == 0) & (f == 0))
    def _():
        obuf[1] = jnp.zeros(obuf.shape[1:], obuf.dtype)
        for half in range(2):
            spare = pltpu.make_async_copy(
                obuf.at[1],
                y_hbm.at[pl.ds((TOP_K * n_tokens + half * MOE_ROWS) * LANE_TILES, MOE_ROWS * LANE_TILES)],
                ssem.at[half])
            spare.start()
            spare.wait()
        for b in (-1, 0, 1):
            idx_copy(b).start()
        idx_copy(-1).wait()
        idx_copy(0).wait()
        for_rows(lambda step, u: gather(0, step, u).start())

    @pl.when((f == 0) & (j <= nu))
    def _():
        wait_gathers(j)

        @pl.when(j >= 1)
        def _():
            wait_scatters(j - 2)

        idx_copy(j + 1).wait()

        @pl.when(j < nu)
        def _():
            idx_copy(j + 2).start()

    @pl.when(j < nu)
    def _():
        slot = lax.rem(j, 2)

        @pl.when(f == 0)
        def _():
            for c in range(LANE_TILES):
                xb_scr[:, c * 128:(c + 1) * 128] = (
                    xbuf[slot, pl.ds(c, MOE_ROWS, stride=LANE_TILES), :].astype(BF16))
            acc_scr[...] = jnp.zeros_like(acc_scr)

        for u in range(rows_per_step):
            gather(j + 1, f, u).start()
            scatter(j - 1, f, u).start()
        xb = xb_scr[...]
        a = jnp.dot(xb, wg_ref[...].astype(BF16), preferred_element_type=F32)
        u_ = jnp.dot(xb, wu_ref[...].astype(BF16), preferred_element_type=F32)
        act = (jax.nn.silu(a) * u_).astype(BF16)
        acc_scr[...] += jnp.dot(act, wd_ref[...].astype(BF16), preferred_element_type=F32)

        @pl.when(f == nf - 1)
        def _():
            for c in range(LANE_TILES):
                obuf[slot, pl.ds(c, MOE_ROWS, stride=LANE_TILES), :] = acc_scr[:, c * 128:(c + 1) * 128]

    @pl.when(j == nu)
    def _():
        for_rows(lambda step, u: scatter(j - 1, step, u).start(), steps=[f])

        @pl.when(f == nf - 1)
        def _():
            wait_scatters(j - 1)


def _experts(block_expert, n_used, idx, h3, wg, wu, wd, li):
    n = h3.shape[0] // LANE_TILES
    d = LANE_TILES * 128
    dff = wg.shape[3]
    tf = dff // MOE_STEPS
    n_blocks = idx.shape[0] - 3

    def fcol(j, f, nu):
        return jnp.where(j < nu[0], f, MOE_STEPS - 1)

    def eblk(j, be, nu):
        return be[jnp.minimum(j, nu[0] - 1)]

    grid_spec = pltpu.PrefetchScalarGridSpec(
        num_scalar_prefetch=2,
        grid=(n_blocks + 1, MOE_STEPS),
        in_specs=[
            pl.BlockSpec(memory_space=pl.ANY),
            pl.BlockSpec(memory_space=pl.ANY),
            pl.BlockSpec((None, None, d, tf), lambda j, f, be, nu: (li, eblk(j, be, nu), 0, fcol(j, f, nu))),
            pl.BlockSpec((None, None, d, tf), lambda j, f, be, nu: (li, eblk(j, be, nu), 0, fcol(j, f, nu))),
            pl.BlockSpec((None, None, tf, d), lambda j, f, be, nu: (li, eblk(j, be, nu), fcol(j, f, nu), 0)),
        ],
        out_specs=pl.BlockSpec(memory_space=pl.ANY),
        scratch_shapes=[
            pltpu.SMEM((4, 2 * MOE_STEPS, MOE_ROWS // MOE_STEPS), jnp.int32),
            pltpu.VMEM((2, MOE_ROWS * LANE_TILES, 128), F32),
            pltpu.VMEM((2, MOE_ROWS * LANE_TILES, 128), F32),
            pltpu.VMEM((MOE_ROWS, d), BF16),
            pltpu.VMEM((MOE_ROWS, d), F32),
            pltpu.SemaphoreType.DMA((4,)),
            pltpu.SemaphoreType.DMA((2,)),
            pltpu.SemaphoreType.DMA((2,)),
        ],
    )
    return pl.pallas_call(
        functools.partial(_experts_kernel, n_tokens=n),
        grid_spec=grid_spec,
        out_shape=jax.ShapeDtypeStruct(((TOP_K * n + 2 * MOE_ROWS) * LANE_TILES, 128), F32),
        compiler_params=_params(("arbitrary", "arbitrary")),
        name="experts",
    )(block_expert, n_used, idx, h3, wg, wu, wd)


def _combine_kernel(h_ref, rg_ref, gf_ref, y0_ref, y1_ref, o_ref, *, final_norm):
    gates = rg_ref[...].T
    g0, g1 = gates[:, 0:1], gates[:, 1:2]
    tm = h_ref.shape[0]
    out = jnp.concatenate([g0 * y0_ref[pl.ds(c, tm, stride=LANE_TILES), :]
                           + g1 * y1_ref[pl.ds(c, tm, stride=LANE_TILES), :]
                           for c in range(LANE_TILES)], axis=1)
    out = h_ref[...] + out
    if final_norm:
        out = _rms(out, gf_ref[...])
    o_ref[...] = out


def _combine(h, rg, gain_final, y, *, tm, final_norm):
    n, d = h.shape
    return pl.pallas_call(
        functools.partial(_combine_kernel, final_norm=final_norm),
        grid=(n // tm,),
        in_specs=[
            pl.BlockSpec((tm, d), lambda i: (i, 0)),
            pl.BlockSpec((8, tm), lambda i: (0, i)),
            pl.BlockSpec((1, d), lambda i: (0, 0)),
            pl.BlockSpec((tm * LANE_TILES, 128), lambda i: (i, 0)),
            pl.BlockSpec((tm * LANE_TILES, 128), lambda i: (i + n // tm, 0)),
        ],
        out_specs=pl.BlockSpec((tm, d), lambda i: (i, 0)),
        out_shape=jax.ShapeDtypeStruct((n, d), F32),
        compiler_params=_params(("parallel",)),
        name="combine",
    )(h, rg, gain_final, y, y)


def _moe(h, gain, w_router, wg, wu, wd, li, gain_final, *, final_norm, tm_route, tm_rows):
    n, d = h.shape
    wr = jnp.zeros((d, 128), F32).at[:, :N_EXPERTS].set(w_router.astype(F32))
    w1 = wr.astype(BF16)
    w2 = (wr - w1.astype(F32)).astype(BF16)
    tri = jnp.asarray(np.triu(np.ones((tm_route, tm_route), np.float32), 1), BF16)
    ri, rg, cnt, hn3 = _router(h, gain, w1, w2, tri, tm=tm_route)

    counts = cnt[:, 0].astype(jnp.int32)
    padded = (counts + MOE_ROWS - 1) // MOE_ROWS * MOE_ROWS
    padded_end = jnp.cumsum(padded)
    padded_start = padded_end - padded
    n_blocks = -(-(n * TOP_K + N_EXPERTS * (MOE_ROWS - 1)) // MOE_ROWS)
    block_start = jnp.arange(n_blocks, dtype=jnp.int32) * MOE_ROWS
    block_expert = jnp.minimum(
        jnp.sum(block_start[:, None] >= padded_end[None, :], axis=1), N_EXPERTS - 1).astype(jnp.int32)
    n_used = (padded_end[-1:] // MOE_ROWS).astype(jnp.int32)
    onehot = ri[0:TOP_K, :, None] == jnp.arange(N_EXPERTS, dtype=jnp.int32)
    dest = jnp.sum(jnp.where(onehot, padded_start, 0), axis=-1) + ri[TOP_K:2 * TOP_K]

    n_idx = (n_blocks + 3) * MOE_ROWS
    pos = jnp.arange(n_idx, dtype=jnp.int32)
    spare = TOP_K * n + (pos // MOE_ROWS % 2) * MOE_ROWS + pos % MOE_ROWS
    hit = jnp.zeros((n_idx,), jnp.int32).at[dest.reshape(-1) + MOE_ROWS].add(
        jnp.arange(1, TOP_K * n + 1, dtype=jnp.int32), unique_indices=True)
    target = jnp.where(hit > 0, hit - 1, spare)
    token = jnp.where(target < TOP_K * n, target % n, 0)
    per_step = MOE_ROWS // MOE_STEPS
    idx = jnp.concatenate([token.reshape(-1, MOE_STEPS, per_step),
                           target.reshape(-1, MOE_STEPS, per_step)], axis=1)

    y = _experts(block_expert, n_used, idx, hn3, wg, wu, wd, li)
    return _combine(h, rg, gain_final, y, tm=tm_rows, final_norm=final_norm)


def kernel(x, mem, w_in, hgrn_lower_bound, hgrn_out_gain, attn_rel_bias, attn_out_gain, w_out,
           norm_mix, norm_mem_q, norm_mem_kv, w_mem_q, w_mem_kv, w_mem_o, norm_ffn,
           w_ffn_gate, w_ffn_up, w_ffn_down, w_router, w_exp_gate, w_exp_up, w_exp_down, norm_final):
    batch, seq, d = x.shape
    mem_len = mem.shape[1]
    depth = w_in.shape[0]
    n = batch * seq
    assert seq % 512 == 0 and seq >= ATTN_WIN and n % 1024 == 0

    lb_probs = jax.nn.softmax(hgrn_lower_bound.astype(F32), axis=0)
    lower_bounds = jnp.cumsum(lb_probs, axis=0) - lb_probs[0]

    h = x.reshape(n, d)
    mem2 = mem.reshape(batch * mem_len, d)
    row = lambda v: v.reshape(1, -1).astype(F32)
    for layer in range(depth):
        proj = _norm_matmul(h, row(norm_mix[layer]), w_in[layer].astype(BF16),
                            tm=1024, col_chunk=512, name="in_proj")
        y_hgrn = _hgrn(proj, row(lower_bounds[layer]), row(hgrn_out_gain[layer]),
                       batch=batch, seq=seq, ts=512)
        y_attn = _attn(proj, _attn_bias(attn_rel_bias[layer]), row(attn_out_gain[layer]),
                       batch=batch, seq=seq)
        kv = _norm_matmul(mem2, row(norm_mem_kv[layer]), w_mem_kv[layer].astype(BF16),
                          tm=min(512, batch * mem_len), col_chunk=512, name="mem_kv")
        h = _post(h, y_hgrn, y_attn, w_out[layer].astype(BF16), row(norm_mem_q[layer]),
                  w_mem_q[layer].astype(BF16), kv, w_mem_o[layer].astype(BF16),
                  seq=seq, mem_len=mem_len, tm=512)
        j = layer // 2
        if layer % 2 == 0:
            h = _ffn(h, row(norm_ffn[layer]), w_ffn_gate[j].astype(BF16), w_ffn_up[j].astype(BF16),
                     w_ffn_down[j].astype(BF16), tm=1024, tf=256)
        else:
            last = layer == depth - 1
            h = _moe(h, row(norm_ffn[layer]), w_router[j], w_exp_gate, w_exp_up, w_exp_down, j, row(norm_final),
                     final_norm=last, tm_route=512, tm_rows=256)
    if depth % 2 == 1:
        h = _norm_only(h, row(norm_final))
    return h.reshape(batch, seq, d)


def _norm_only_kernel(x_ref, g_ref, o_ref):
    o_ref[...] = _rms(x_ref[...], g_ref[...])


def _norm_only(x, gain, tm=1024):
    n, d = x.shape
    return pl.pallas_call(
        _norm_only_kernel,
        grid=(n // tm,),
        in_specs=[pl.BlockSpec((tm, d), lambda i: (i, 0)), pl.BlockSpec((1, d), lambda i: (0, 0))],
        out_specs=pl.BlockSpec((tm, d), lambda i: (i, 0)),
        out_shape=jax.ShapeDtypeStruct((n, d), F32),
        compiler_params=_params(("parallel",)),
        name="final_norm",
    )(x, gain)
```

```python
import functools

import numpy as np
import jax
import jax.numpy as jnp
from jax import lax
from jax.experimental import pallas as pl
from jax.experimental.pallas import tpu as pltpu

F32 = jnp.float32
BF16 = jnp.bfloat16

EPS = 1e-6
CHUNK = 64
LEFT_CHUNKS = 8
MAX_REL = 128
HGRN_HEADS = 4
HGRN_HEAD_DIM = 128
HGRN_WIDTH = HGRN_HEADS * HGRN_HEAD_DIM
ATTN_HEADS = 8
ATTN_HEAD_DIM = 64
ATTN_WIDTH = ATTN_HEADS * ATTN_HEAD_DIM
MEM_HEADS = 4
N_EXPERTS = 8
TOP_K = 2

SUB = 16
FAST = 32
FAST_MAX_DECAY = 60.0
ATTN_QBLK = 4 * CHUNK
ATTN_WIN = ATTN_QBLK + LEFT_CHUNKS * CHUNK
MOE_STEPS = 7
LANE_TILES = 8
MOE_ROWS = 144 * MOE_STEPS
NEG = -1e30
LOG2E = 1.4426950408889634

VMEM_LIMIT = 56 * 1024 * 1024

ROWS_IN_PROJ = 1024
ROWS_HGRN = 512
ROWS_POST = 1024
ROWS_MEM_KV = 512
ROWS_FFN = 1024
ROWS_ROUTER = 512
ROWS_COMBINE = 512
PROJ_COLS = 512
FFN_COLS = 256


def _params(sem):
    return pltpu.CompilerParams(dimension_semantics=sem, vmem_limit_bytes=VMEM_LIMIT)


def _rms(x, gain):
    ms = jnp.mean(x * x, axis=-1, keepdims=True)
    return x * lax.rsqrt(ms + EPS) * gain


def _softmax_rows(s):
    m = jnp.max(s, axis=-1, keepdims=True)
    e = jnp.exp(s - m)
    return e, jnp.sum(e, axis=-1, keepdims=True)


def _norm_matmul_kernel(x_ref, g_ref, w_ref, o_ref, *, col_chunk):
    xn = _rms(x_ref[...], g_ref[...]).astype(BF16)
    n_out = o_ref.shape[1]
    for c in range(n_out // col_chunk):
        sl = slice(c * col_chunk, (c + 1) * col_chunk)
        o_ref[:, sl] = jnp.dot(xn, w_ref[:, sl], preferred_element_type=F32).astype(o_ref.dtype)


def _norm_matmul(x, gain, w, *, tm, col_chunk, name):
    n, d = x.shape
    n_out = w.shape[1]
    return pl.pallas_call(
        functools.partial(_norm_matmul_kernel, col_chunk=col_chunk),
        grid=(n // tm,),
        in_specs=[
            pl.BlockSpec((tm, d), lambda i: (i, 0)),
            pl.BlockSpec((1, d), lambda i: (0, 0)),
            pl.BlockSpec((d, n_out), lambda i: (0, 0)),
        ],
        out_specs=pl.BlockSpec((tm, n_out), lambda i: (i, 0)),
        out_shape=jax.ShapeDtypeStruct((n, n_out), BF16),
        compiler_params=_params(("parallel",)),
        name=name,
    )(x, gain, w)


def _hgrn_exact_tile(q_ref, i_ref, st_ref, lf_scr, k_scr, bl_scr, kc_scr, vc_scr, o_scr, ts):
    hd = HGRN_HEAD_DIM
    r = lax.broadcasted_iota(jnp.int32, (SUB, SUB), 0)
    c = lax.broadcasted_iota(jnp.int32, (SUB, SUB), 1)
    ltri = jnp.where(r >= c, 1.0, 0.0).astype(BF16)
    rowid = lax.broadcasted_iota(jnp.int32, (SUB, hd), 0)

    def sub_chunk(j, carry):
        r0 = pl.multiple_of(j * SUB, SUB)
        lf = lf_scr[pl.ds(r0, SUB), :]
        hi = lf.astype(BF16)
        lo = (lf - hi.astype(F32)).astype(BF16)
        bl = (jnp.dot(ltri, hi, preferred_element_type=F32)
              + jnp.dot(ltri, lo, preferred_element_type=F32))
        bl_scr[...] = bl
        kc_scr[...] = k_scr[pl.ds(r0, SUB), :]
        vc_scr[...] = i_ref[pl.ds(r0, SUB), :].astype(F32)
        qf = q_ref[pl.ds(r0, SUB), :].astype(F32)
        for h in range(HGRN_HEADS):
            sl = slice(h * hd, (h + 1) * hd)
            blh = bl[:, sl]
            qh = qf[:, sl]
            st = st_ref[h]
            o = lax.dot_general((qh * jnp.exp(blh)).astype(BF16), st.astype(BF16),
                                (((1,), (1,)), ((), ())), preferred_element_type=F32)
            for s in range(SUB):
                bs = bl_scr[s:s + 1, sl]
                ks = kc_scr[s:s + 1, sl]
                vs = vc_scr[s:s + 1, sl]
                e = jnp.exp(jnp.minimum(blh - bs, 0.0))
                a = jnp.sum(qh * e * ks, axis=-1, keepdims=True)
                a = jnp.where(rowid >= s, a, 0.0)
                o = o + a * vs
            o_scr[pl.ds(r0, SUB), sl] = o
            bend = bl_scr[SUB - 1:SUB, sl]
            kdec = (kc_scr[:, sl] * jnp.exp(bend - blh)).astype(BF16)
            vt = vc_scr[:, sl].T.astype(BF16)
            st_ref[h] = st * jnp.exp(bend) + jnp.dot(vt, kdec, preferred_element_type=F32)
        return carry

    lax.fori_loop(0, ts // SUB, sub_chunk, 0)


def _hgrn_fast_tile(q_ref, i_ref, st_ref, k_scr, b_scr, o_scr, ts):
    hd = HGRN_HEAD_DIM
    r = lax.broadcasted_iota(jnp.int32, (CHUNK, CHUNK), 0)
    c = lax.broadcasted_iota(jnp.int32, (CHUNK, CHUNK), 1)
    causal = r >= c
    first_half = lax.broadcasted_iota(jnp.int32, (CHUNK, HGRN_WIDTH), 0) < FAST

    def block(j, carry):
        r0 = pl.multiple_of(j * CHUNK, CHUNK)
        b = b_scr[pl.ds(r0, CHUNK), :]
        b_mid = b[FAST - 1:FAST, :]
        rel = b - jnp.where(first_half, b_mid, 0.0)
        e_mid = jnp.exp(b_mid)
        e_tail = jnp.exp(b[CHUNK - 1:CHUNK, :])
        kdf = k_scr[pl.ds(r0, CHUNK), :] * jnp.exp(-rel)
        qd = (q_ref[pl.ds(r0, CHUNK), :].astype(F32) * jnp.exp(rel)).astype(BF16)
        kd = kdf.astype(BF16)
        v = i_ref[pl.ds(r0, CHUNK), :]
        kdec = (kdf * e_tail).astype(BF16)
        vt = v.astype(F32).T.astype(BF16)
        for h in range(HGRN_HEADS):
            sl = slice(h * hd, (h + 1) * hd)
            st = st_ref[h]
            a = lax.dot_general(qd[:, sl], kd[:, sl], (((1,), (1,)), ((), ())),
                                preferred_element_type=F32)
            a = jnp.where(causal, a, 0.0).astype(BF16)
            st_mid = (st * e_mid[:, sl]).astype(BF16)
            o_scr[pl.ds(r0, CHUNK), sl] = (
                jnp.dot(a, v[:, sl], preferred_element_type=F32)
                + lax.dot_general(qd[:, sl], st_mid, (((1,), (1,)), ((), ())),
                                  preferred_element_type=F32))
            st_ref[h] = (st * (e_mid[:, sl] * e_tail[:, sl])
                         + jnp.dot(vt[sl, :], kdec[:, sl], preferred_element_type=F32))
        return carry

    lax.fori_loop(0, ts // CHUNK, block, 0, unroll=True)


def _hgrn_kernel(q_ref, f_ref, i_ref, g_ref, lb_ref, gain_ref, tri_ref, o_ref,
                 st_ref, lf_scr, k_scr, b_scr, bl_scr, kc_scr, vc_scr, o_scr, *, ts):
    hd = HGRN_HEAD_DIM

    @pl.when(pl.program_id(1) == 0)
    def _():
        st_ref[...] = jnp.zeros_like(st_ref)

    lb = lb_ref[...]
    f = lb + (1.0 - lb) * jax.nn.sigmoid(f_ref[...].astype(F32))
    lf = jnp.log(f)
    lf_scr[...] = lf
    k_scr[...] = 1.0 - f
    hi = lf.astype(BF16)
    lo = (lf - hi.astype(F32)).astype(BF16)
    nblk = ts // CHUNK
    side = lambda x: jnp.concatenate([x[j * CHUNK:(j + 1) * CHUNK, :] for j in range(nblk)], axis=1)
    b = (jnp.dot(tri_ref[...], side(hi), preferred_element_type=F32)
         + jnp.dot(tri_ref[...], side(lo), preferred_element_type=F32))
    w = lf.shape[1]
    for j in range(nblk):
        b_scr[j * CHUNK:(j + 1) * CHUNK, :] = b[:, j * w:(j + 1) * w]
    mild = jnp.min(b) >= -FAST_MAX_DECAY

    @pl.when(mild)
    def _():
        _hgrn_fast_tile(q_ref, i_ref, st_ref, k_scr, b_scr, o_scr, ts)

    @pl.when(jnp.logical_not(mild))
    def _():
        _hgrn_exact_tile(q_ref, i_ref, st_ref, lf_scr, k_scr, bl_scr, kc_scr, vc_scr, o_scr, ts)

    gate = jax.nn.silu(g_ref[...].astype(F32)) * gain_ref[...]
    for h in range(HGRN_HEADS):
        sl = slice(h * hd, (h + 1) * hd)
        o = o_scr[:, sl]
        o = o * lax.rsqrt(jnp.mean(o * o, axis=-1, keepdims=True) + EPS)
        o_ref[:, sl] = (o * gate[:, sl]).astype(o_ref.dtype)


def _hgrn(proj, lb, gain, *, batch, seq, ts):
    n = batch * seq
    w = HGRN_WIDTH
    steps = seq // ts
    rows = np.arange(CHUNK)
    tri = jnp.asarray((rows[:, None] // FAST == rows[None, :] // FAST) & (rows[:, None] >= rows[None, :]), BF16)

    def col(j):
        return pl.BlockSpec((ts, w), lambda b, i, j=j: (b * steps + i, j))

    vec = pl.BlockSpec((1, w), lambda b, i: (0, 0))
    return pl.pallas_call(
        functools.partial(_hgrn_kernel, ts=ts),
        grid=(batch, steps),
        in_specs=[col(0), col(1), col(2), col(3), vec, vec, pl.BlockSpec((CHUNK, CHUNK), lambda b, i: (0, 0))],
        out_specs=pl.BlockSpec((ts, w), lambda b, i: (b * steps + i, 0)),
        out_shape=jax.ShapeDtypeStruct((n, w), BF16),
        scratch_shapes=[
            pltpu.VMEM((HGRN_HEADS, HGRN_HEAD_DIM, HGRN_HEAD_DIM), F32),
            pltpu.VMEM((ts, w), F32),
            pltpu.VMEM((ts, w), F32),
            pltpu.VMEM((ts, w), F32),
            pltpu.VMEM((SUB, w), F32),
            pltpu.VMEM((SUB, w), F32),
            pltpu.VMEM((SUB, w), F32),
            pltpu.VMEM((ts, w), F32),
        ],
        compiler_params=_params(("parallel", "arbitrary")),
        name="hgrn",
    )(proj, proj, proj, proj, lb, gain, tri)


def _attn_block(q, kwin, vwin, bias_ref, lo, gain, o_ref):
    qs = (q.astype(F32) * (ATTN_HEAD_DIM ** -0.5 * LOG2E)).astype(BF16)
    lane = lax.broadcasted_iota(jnp.int32, (ATTN_QBLK, 128), 1)
    first = lane < ATTN_HEAD_DIM
    ones = jnp.ones((kwin.shape[0], 128), BF16)
    outs = []
    for p in range(ATTN_HEADS // 2):
        sl = slice(p * 128, (p + 1) * 128)
        qp, kp, vp = qs[:, sl], kwin[:, sl], vwin[:, sl]
        zero = jnp.zeros_like(qp)
        q2 = jnp.concatenate([jnp.where(first, qp, zero), jnp.where(first, zero, qp)], axis=0)
        s = lax.dot_general(q2, kp, (((1,), (1,)), ((), ())), preferred_element_type=F32)
        s = s + bias_ref[p, :, lo:]
        e = jnp.exp2(s - jnp.max(s, axis=-1, keepdims=True)).astype(BF16)
        oa = jnp.dot(e, jnp.concatenate([vp, ones], axis=1), preferred_element_type=F32)
        o2 = oa[:, :128] / oa[:, 128:]
        outs.append(jnp.where(first, o2[:ATTN_QBLK], o2[ATTN_QBLK:]))
    o = jnp.concatenate(outs, axis=1)
    o_ref[...] = _rms(o, gain).astype(o_ref.dtype)


def _attn_kernel(q_ref, k_ref, v_ref, bias_ref, gain_ref, o_ref):
    i = pl.program_id(1)
    gain = gain_ref[...]
    left = LEFT_CHUNKS * CHUNK

    for blk in range(left // ATTN_QBLK):
        @pl.when(i == blk)
        def _(blk=blk):
            w = (blk + 1) * ATTN_QBLK
            _attn_block(q_ref[...], k_ref[0:w, :], v_ref[0:w, :], bias_ref, ATTN_WIN - w, gain, o_ref)

    @pl.when(i >= left // ATTN_QBLK)
    def _():
        start = pl.multiple_of(i * ATTN_QBLK - left, ATTN_QBLK)
        _attn_block(q_ref[...], k_ref[pl.ds(start, ATTN_WIN), :], v_ref[pl.ds(start, ATTN_WIN), :],
                    bias_ref, 0, gain, o_ref)


def _attn_bias(rel_bias):
    q, w = ATTN_QBLK, ATTN_WIN
    period = q + w - 1
    x = np.arange(period)
    x = np.where(x < w, x, x - period)
    vec = rel_bias.astype(F32)[:, np.clip(LEFT_CHUNKS * CHUNK - x, -MAX_REL, MAX_REL) + MAX_REL]
    bias = jnp.tile(vec, (1, q))[:, :q * (period - 1)].reshape(-1, q, period - 1)[:, :, :w]
    qi = np.arange(q)[:, None]
    kj = np.arange(w)[None, :]
    dchunk = qi // CHUNK + LEFT_CHUNKS - kj // CHUNK
    in_band = (dchunk >= 0) & (dchunk <= LEFT_CHUNKS)
    bias = jnp.where(in_band[None], bias * LOG2E, NEG)
    return bias.reshape(-1, 2 * q, w)


def _attn(proj, bias, gain, *, batch, seq):
    n = batch * seq
    w = ATTN_WIDTH
    steps = seq // ATTN_QBLK
    qcol = 4 * HGRN_WIDTH // w
    return pl.pallas_call(
        _attn_kernel,
        grid=(batch, steps),
        in_specs=[
            pl.BlockSpec((ATTN_QBLK, w), lambda b, i: (b * steps + i, qcol)),
            pl.BlockSpec((seq, w), lambda b, i: (b, qcol + 1)),
            pl.BlockSpec((seq, w), lambda b, i: (b, qcol + 2)),
            pl.BlockSpec((ATTN_HEADS // 2, 2 * ATTN_QBLK, ATTN_WIN), lambda b, i: (0, 0, 0)),
            pl.BlockSpec((1, w), lambda b, i: (0, 0)),
        ],
        out_specs=pl.BlockSpec((ATTN_QBLK, w), lambda b, i: (b * steps + i, 0)),
        out_shape=jax.ShapeDtypeStruct((n, w), BF16),
        compiler_params=_params(("parallel", "arbitrary")),
        name="attn",
    )(proj, proj, proj, bias, gain)


def _post_kernel(h_ref, yh_ref, ya_ref, wout_ref, gq_ref, wq_ref, kv_ref, wo_ref, o_ref):
    d = h_ref.shape[1]
    hw = yh_ref.shape[1]
    h1 = (h_ref[...]
          + jnp.dot(yh_ref[...], wout_ref[0:hw, :], preferred_element_type=F32)
          + jnp.dot(ya_ref[...], wout_ref[hw:, :], preferred_element_type=F32))
    hn = _rms(h1, gq_ref[...]).astype(BF16)
    dh = d // MEM_HEADS
    q = (jnp.dot(hn, wq_ref[...], preferred_element_type=F32) * (dh ** -0.5)).astype(BF16)
    outs = []
    for hd in range(MEM_HEADS):
        sl = slice(hd * dh, (hd + 1) * dh)
        k = kv_ref[:, sl]
        v = kv_ref[:, d + hd * dh:d + (hd + 1) * dh]
        s = lax.dot_general(q[:, sl], k, (((1,), (1,)), ((), ())), preferred_element_type=F32)
        e, l = _softmax_rows(s)
        outs.append((jnp.dot(e.astype(BF16), v, preferred_element_type=F32) / l).astype(BF16))
    o = jnp.concatenate(outs, axis=1)
    o_ref[...] = h1 + jnp.dot(o, wo_ref[...], preferred_element_type=F32)


def _post(h, yh, ya, w_out, gq, wq, kv, wo, *, seq, mem_len, tm):
    n, d = h.shape
    per_batch = seq // tm
    const = lambda i: (0, 0)
    resident = dict(pipeline_mode=pl.Buffered(1))
    return pl.pallas_call(
        _post_kernel,
        grid=(n // tm,),
        in_specs=[
            pl.BlockSpec((tm, d), lambda i: (i, 0)),
            pl.BlockSpec((tm, yh.shape[1]), lambda i: (i, 0)),
            pl.BlockSpec((tm, ya.shape[1]), lambda i: (i, 0)),
            pl.BlockSpec(w_out.shape, const, **resident),
            pl.BlockSpec((1, d), const),
            pl.BlockSpec(wq.shape, const, **resident),
            pl.BlockSpec((mem_len, 2 * d), lambda i: (i // per_batch, 0)),
            pl.BlockSpec(wo.shape, const, **resident),
        ],
        out_specs=pl.BlockSpec((tm, d), lambda i: (i, 0)),
        out_shape=jax.ShapeDtypeStruct((n, d), F32),
        compiler_params=_params(("parallel",)),
        name="post",
    )(h, yh, ya, w_out, gq, wq, kv, wo)


def _ffn_kernel(h_ref, g_ref, wg_ref, wu_ref, wd_ref, o_ref, xn_scr, acc_scr, *, tf):
    xn_scr[...] = _rms(h_ref[...], g_ref[...]).astype(BF16)
    acc_scr[...] = jnp.zeros_like(acc_scr)

    def slab(c, carry):
        c0 = pl.multiple_of(c * tf, tf)
        xn = xn_scr[...]
        a = jnp.dot(xn, wg_ref[:, pl.ds(c0, tf)], preferred_element_type=F32)
        u = jnp.dot(xn, wu_ref[:, pl.ds(c0, tf)], preferred_element_type=F32)
        act = (jax.nn.silu(a) * u).astype(BF16)
        acc_scr[...] += jnp.dot(act, wd_ref[pl.ds(c0, tf), :], preferred_element_type=F32)
        return carry

    lax.fori_loop(0, wg_ref.shape[1] // tf, slab, 0)
    o_ref[...] = h_ref[...] + acc_scr[...]


def _ffn(h, gain, wg, wu, wd, *, tm, tf):
    n, d = h.shape
    dff = wg.shape[1]
    resident = dict(pipeline_mode=pl.Buffered(1))
    return pl.pallas_call(
        functools.partial(_ffn_kernel, tf=tf),
        grid=(n // tm,),
        in_specs=[
            pl.BlockSpec((tm, d), lambda i: (i, 0)),
            pl.BlockSpec((1, d), lambda i: (0, 0)),
            pl.BlockSpec((d, dff), lambda i: (0, 0), **resident),
            pl.BlockSpec((d, dff), lambda i: (0, 0), **resident),
            pl.BlockSpec((dff, d), lambda i: (0, 0), **resident),
        ],
        out_specs=pl.BlockSpec((tm, d), lambda i: (i, 0)),
        out_shape=jax.ShapeDtypeStruct((n, d), F32),
        scratch_shapes=[pltpu.VMEM((tm, d), BF16), pltpu.VMEM((tm, d), F32)],
        compiler_params=_params(("parallel",)),
        name="ffn",
    )(h, gain, wg, wu, wd)


def _router_kernel(h_ref, g_ref, w1_ref, w2_ref, tri_ref, ri_ref, rg_ref, cnt_ref, hn_ref, carry_scr):
    i = pl.program_id(0)
    tm = h_ref.shape[0]

    @pl.when(i == 0)
    def _():
        carry_scr[...] = jnp.zeros_like(carry_scr)

    hn = _rms(h_ref[...], g_ref[...])
    for c in range(LANE_TILES):
        hn_ref[pl.ds(c, tm, stride=LANE_TILES), :] = hn[:, c * 128:(c + 1) * 128]
    a1 = hn.astype(BF16)
    a2 = (hn - a1.astype(F32)).astype(BF16)
    lg = (jnp.dot(a1, w1_ref[...], preferred_element_type=F32)
          + jnp.dot(a1, w2_ref[...], preferred_element_type=F32)
          + jnp.dot(a2, w1_ref[...], preferred_element_type=F32))
    lgt = lg.T[0:N_EXPERTS, :]
    ids = lax.broadcasted_iota(jnp.int32, (N_EXPERTS, tm), 0)
    m1 = jnp.max(lgt, axis=0, keepdims=True)
    i1 = jnp.min(jnp.where(lgt == m1, ids, N_EXPERTS), axis=0, keepdims=True)
    rest = jnp.where(ids == i1, -jnp.inf, lgt)
    m2 = jnp.max(rest, axis=0, keepdims=True)
    i2 = jnp.min(jnp.where(rest == m2, ids, N_EXPERTS), axis=0, keepdims=True)
    e = jnp.exp(m2 - m1)
    g1 = 1.0 / (1.0 + e)
    g2 = e * g1
    hit1 = ids == i1
    hit2 = ids == i2
    onehot = jnp.where(hit1 | hit2, 1.0, 0.0)
    before = jnp.dot(onehot.astype(BF16), tri_ref[...], preferred_element_type=F32) + carry_scr[...][:, 0:1]
    r1 = jnp.sum(jnp.where(hit1, before, 0.0), axis=0, keepdims=True)
    r2 = jnp.sum(jnp.where(hit2, before, 0.0), axis=0, keepdims=True)
    carry_scr[...] = carry_scr[...] + jnp.sum(onehot, axis=1, keepdims=True)
    cnt_ref[...] = carry_scr[...]
    row = lax.broadcasted_iota(jnp.int32, (8, tm), 0)
    ri_ref[...] = jnp.where(row == 0, i1, jnp.where(row == 1, i2, jnp.where(
        row == 2, r1.astype(jnp.int32), jnp.where(row == 3, r2.astype(jnp.int32), 0))))
    rg_ref[...] = jnp.where(row == 0, g1, jnp.where(row == 1, g2, 0.0))


def _router(h, gain, w1, w2, tri, *, tm):
    n, d = h.shape
    return pl.pallas_call(
        _router_kernel,
        grid=(n // tm,),
        in_specs=[
            pl.BlockSpec((tm, d), lambda i: (i, 0)),
            pl.BlockSpec((1, d), lambda i: (0, 0)),
            pl.BlockSpec(w1.shape, lambda i: (0, 0)),
            pl.BlockSpec(w2.shape, lambda i: (0, 0)),
            pl.BlockSpec((tm, tm), lambda i: (0, 0)),
        ],
        out_specs=[
            pl.BlockSpec((8, tm), lambda i: (0, i)),
            pl.BlockSpec((8, tm), lambda i: (0, i)),
            pl.BlockSpec((N_EXPERTS, 128), lambda i: (0, 0)),
            pl.BlockSpec((tm * LANE_TILES, 128), lambda i: (i, 0)),
        ],
        out_shape=[
            jax.ShapeDtypeStruct((8, n), jnp.int32),
            jax.ShapeDtypeStruct((8, n), F32),
            jax.ShapeDtypeStruct((N_EXPERTS, 128), F32),
            jax.ShapeDtypeStruct((n * LANE_TILES, 128), F32),
        ],
        scratch_shapes=[pltpu.VMEM((N_EXPERTS, 128), F32)],
        compiler_params=_params(("arbitrary",)),
        name="router",
    )(h, gain, w1, w2, tri)


def _experts_kernel(be_ref, nu_ref, idx_hbm, h_hbm, wg_ref, wu_ref, wd_ref, y_hbm,
                    idx_smem, xbuf, obuf, xb_scr, acc_scr, isem, gsem, ssem, *, n_tokens):
    j = pl.program_id(0)
    f = pl.program_id(1)
    nf = pl.num_programs(1)
    nu = nu_ref[0]
    rows_per_step = MOE_ROWS // MOE_STEPS

    def idx_copy(b):
        s = lax.rem(b + 1, 4)
        return pltpu.make_async_copy(idx_hbm.at[b + 1], idx_smem.at[s], isem.at[s])

    def gather(b, step, u):
        tok = idx_smem[lax.rem(b + 1, 4), step, u]
        s = lax.rem(b + 2, 2)
        r = step * rows_per_step + u
        return pltpu.make_async_copy(h_hbm.at[pl.ds(tok * LANE_TILES, LANE_TILES)],
                                     xbuf.at[s, pl.ds(r * LANE_TILES, LANE_TILES)], gsem.at[s])

    def scatter(b, step, u):
        tgt = idx_smem[lax.rem(b + 1, 4), MOE_STEPS + step, u]
        s = lax.rem(b + 2, 2)
        r = step * rows_per_step + u
        return pltpu.make_async_copy(obuf.at[s, pl.ds(r * LANE_TILES, LANE_TILES)],
                                     y_hbm.at[pl.ds(tgt * LANE_TILES, LANE_TILES)], ssem.at[s])

    def wait_gathers(b):
        s = lax.rem(b + 2, 2)
        pltpu.make_async_copy(h_hbm.at[pl.ds(0, MOE_ROWS * LANE_TILES)], xbuf.at[s], gsem.at[s]).wait()

    def wait_scatters(b):
        s = lax.rem(b + 2, 2)
        pltpu.make_async_copy(obuf.at[s], y_hbm.at[pl.ds(0, MOE_ROWS * LANE_TILES)], ssem.at[s]).wait()

    def for_rows(fn, steps=range(MOE_STEPS)):
        for step in steps:
            def body(u, carry, step=step):
                fn(step, u)
                return carry
            lax.fori_loop(0, rows_per_step, body, 0, unroll=16)

    @pl.when((j == 0) & (f == 0))
    def _():
        obuf[1] = jnp.zeros(obuf.shape[1:], obuf.dtype)
        for half in range(2):
            spare = pltpu.make_async_copy(
                obuf.at[1],
                y_hbm.at[pl.ds((TOP_K * n_tokens + half * MOE_ROWS) * LANE_TILES, MOE_ROWS * LANE_TILES)],
                ssem.at[half])
            spare.start()
            spare.wait()
        for b in (-1, 0, 1):
            idx_copy(b).start()
        idx_copy(-1).wait()
        idx_copy(0).wait()
        for_rows(lambda step, u: gather(0, step, u).start())

    @pl.when((f == 0) & (j <= nu))
    def _():
        @pl.when(j == nu)
        def _():
            wait_gathers(j)

        @pl.when(j >= 1)
        def _():
            wait_scatters(j - 2)

        idx_copy(j + 1).wait()

        @pl.when(j < nu)
        def _():
            idx_copy(j + 2).start()

    @pl.when(j < nu)
    def _():
        slot = lax.rem(j, 2)

        @pl.when(f == 0)
        def _():
            acc_scr[...] = jnp.zeros_like(acc_scr)
            wait_gathers(j)
            for c in range(LANE_TILES):
                xb_scr[:, c * 128:(c + 1) * 128] = (
                    xbuf[slot, pl.ds(c, MOE_ROWS, stride=LANE_TILES), :].astype(BF16))

        for u in range(rows_per_step):
            gather(j + 1, f, u).start()
            scatter(j - 1, f, u).start(priority=u % 2)
        xb = xb_scr[...]
        a = jnp.dot(xb, wg_ref[...].astype(BF16), preferred_element_type=F32)
        u_ = jnp.dot(xb, wu_ref[...].astype(BF16), preferred_element_type=F32)
        act = (jax.nn.silu(a) * u_).astype(BF16)
        acc_scr[...] += jnp.dot(act, wd_ref[...].astype(BF16), preferred_element_type=F32)

        @pl.when(f == nf - 1)
        def _():
            for c in range(LANE_TILES):
                obuf[slot, pl.ds(c, MOE_ROWS, stride=LANE_TILES), :] = acc_scr[:, c * 128:(c + 1) * 128]

    @pl.when(j == nu)
    def _():
        for_rows(lambda step, u: scatter(j - 1, step, u).start(), steps=[f])

        @pl.when(f == nf - 1)
        def _():
            wait_scatters(j - 1)


def _experts(block_expert, n_used, idx, h3, wg, wu, wd, li):
    n = h3.shape[0] // LANE_TILES
    d = LANE_TILES * 128
    dff = wg.shape[3]
    tf = dff // MOE_STEPS
    n_blocks = idx.shape[0] - 3

    def fcol(j, f, nu):
        return jnp.where(j < nu[0], f, MOE_STEPS - 1)

    def eblk(j, be, nu):
        return be[jnp.minimum(j, nu[0] - 1)]

    grid_spec = pltpu.PrefetchScalarGridSpec(
        num_scalar_prefetch=2,
        grid=(n_blocks + 1, MOE_STEPS),
        in_specs=[
            pl.BlockSpec(memory_space=pl.ANY),
            pl.BlockSpec(memory_space=pl.ANY),
            pl.BlockSpec((None, None, d, tf), lambda j, f, be, nu: (li, eblk(j, be, nu), 0, fcol(j, f, nu))),
            pl.BlockSpec((None, None, d, tf), lambda j, f, be, nu: (li, eblk(j, be, nu), 0, fcol(j, f, nu))),
            pl.BlockSpec((None, None, tf, d), lambda j, f, be, nu: (li, eblk(j, be, nu), fcol(j, f, nu), 0)),
        ],
        out_specs=pl.BlockSpec(memory_space=pl.ANY),
        scratch_shapes=[
            pltpu.SMEM((4, 2 * MOE_STEPS, MOE_ROWS // MOE_STEPS), jnp.int32),
            pltpu.VMEM((2, MOE_ROWS * LANE_TILES, 128), F32),
            pltpu.VMEM((2, MOE_ROWS * LANE_TILES, 128), F32),
            pltpu.VMEM((MOE_ROWS, d), BF16),
            pltpu.VMEM((MOE_ROWS, d), F32),
            pltpu.SemaphoreType.DMA((4,)),
            pltpu.SemaphoreType.DMA((2,)),
            pltpu.SemaphoreType.DMA((2,)),
        ],
    )
    return pl.pallas_call(
        functools.partial(_experts_kernel, n_tokens=n),
        grid_spec=grid_spec,
        out_shape=jax.ShapeDtypeStruct(((TOP_K * n + 2 * MOE_ROWS) * LANE_TILES, 128), F32),
        compiler_params=_params(("arbitrary", "arbitrary")),
        name="experts",
    )(block_expert, n_used, idx, h3, wg, wu, wd)


def _combine_kernel(h_ref, rg_ref, gf_ref, y0_ref, y1_ref, o_ref, *, final_norm):
    gates = rg_ref[...].T
    g0, g1 = gates[:, 0:1], gates[:, 1:2]
    tm = h_ref.shape[0]
    out = jnp.concatenate([g0 * y0_ref[pl.ds(c, tm, stride=LANE_TILES), :]
                           + g1 * y1_ref[pl.ds(c, tm, stride=LANE_TILES), :]
                           for c in range(LANE_TILES)], axis=1)
    out = h_ref[...] + out
    if final_norm:
        out = _rms(out, gf_ref[...])
    o_ref[...] = out


def _combine(h, rg, gain_final, y, *, tm, final_norm):
    n, d = h.shape
    return pl.pallas_call(
        functools.partial(_combine_kernel, final_norm=final_norm),
        grid=(n // tm,),
        in_specs=[
            pl.BlockSpec((tm, d), lambda i: (i, 0)),
            pl.BlockSpec((8, tm), lambda i: (0, i)),
            pl.BlockSpec((1, d), lambda i: (0, 0)),
            pl.BlockSpec((tm * LANE_TILES, 128), lambda i: (i, 0)),
            pl.BlockSpec((tm * LANE_TILES, 128), lambda i: (i + n // tm, 0)),
        ],
        out_specs=pl.BlockSpec((tm, d), lambda i: (i, 0)),
        out_shape=jax.ShapeDtypeStruct((n, d), F32),
        compiler_params=_params(("parallel",)),
        name="combine",
    )(h, rg, gain_final, y, y)


def _moe(h, gain, w_router, wg, wu, wd, li, gain_final, *, final_norm, tm_route, tm_rows):
    n, d = h.shape
    wr = jnp.zeros((d, 128), F32).at[:, :N_EXPERTS].set(w_router.astype(F32))
    w1 = wr.astype(BF16)
    w2 = (wr - w1.astype(F32)).astype(BF16)
    tri = jnp.asarray(np.triu(np.ones((tm_route, tm_route), np.float32), 1), BF16)
    ri, rg, cnt, hn3 = _router(h, gain, w1, w2, tri, tm=tm_route)

    counts = cnt[:, 0].astype(jnp.int32)
    padded = (counts + MOE_ROWS - 1) // MOE_ROWS * MOE_ROWS
    padded_end = jnp.cumsum(padded)
    padded_start = padded_end - padded
    n_blocks = -(-(n * TOP_K + N_EXPERTS * (MOE_ROWS - 1)) // MOE_ROWS)
    block_start = jnp.arange(n_blocks, dtype=jnp.int32) * MOE_ROWS
    block_expert = jnp.minimum(
        jnp.sum(block_start[:, None] >= padded_end[None, :], axis=1), N_EXPERTS - 1).astype(jnp.int32)
    n_used = (padded_end[-1:] // MOE_ROWS).astype(jnp.int32)
    onehot = ri[0:TOP_K, :, None] == jnp.arange(N_EXPERTS, dtype=jnp.int32)
    dest = jnp.sum(jnp.where(onehot, padded_start, 0), axis=-1) + ri[TOP_K:2 * TOP_K]

    n_idx = (n_blocks + 3) * MOE_ROWS
    pos = jnp.arange(n_idx, dtype=jnp.int32)
    spare = TOP_K * n + (pos // MOE_ROWS % 2) * MOE_ROWS + pos % MOE_ROWS
    hit = jnp.zeros((n_idx,), jnp.int32).at[dest.reshape(-1) + MOE_ROWS].add(
        jnp.arange(1, TOP_K * n + 1, dtype=jnp.int32), unique_indices=True)
    target = jnp.where(hit > 0, hit - 1, spare)
    token = jnp.where(target < TOP_K * n, target % n, 0)
    per_step = MOE_ROWS // MOE_STEPS
    idx = jnp.concatenate([token.reshape(-1, MOE_STEPS, per_step),
                           target.reshape(-1, MOE_STEPS, per_step)], axis=1)

    y = _experts(block_expert, n_used, idx, hn3, wg, wu, wd, li)
    return _combine(h, rg, gain_final, y, tm=tm_rows, final_norm=final_norm)


def kernel(x, mem, w_in, hgrn_lower_bound, hgrn_out_gain, attn_rel_bias, attn_out_gain, w_out,
           norm_mix, norm_mem_q, norm_mem_kv, w_mem_q, w_mem_kv, w_mem_o, norm_ffn,
           w_ffn_gate, w_ffn_up, w_ffn_down, w_router, w_exp_gate, w_exp_up, w_exp_down, norm_final):
    batch, seq, d = x.shape
    mem_len = mem.shape[1]
    depth = w_in.shape[0]
    n = batch * seq
    assert seq % ROWS_POST == 0 and seq >= ATTN_WIN and n % max(ROWS_IN_PROJ, ROWS_FFN) == 0

    lb_probs = jax.nn.softmax(hgrn_lower_bound.astype(F32), axis=0)
    lower_bounds = jnp.cumsum(lb_probs, axis=0) - lb_probs[0]

    h = x.reshape(n, d)
    mem2 = mem.reshape(batch * mem_len, d)
    row = lambda v: v.reshape(1, -1).astype(F32)
    for layer in range(depth):
        proj = _norm_matmul(h, row(norm_mix[layer]), w_in[layer].astype(BF16),
                            tm=ROWS_IN_PROJ, col_chunk=PROJ_COLS, name="in_proj")
        y_hgrn = _hgrn(proj, row(lower_bounds[layer]), row(hgrn_out_gain[layer]),
                       batch=batch, seq=seq, ts=ROWS_HGRN)
        y_attn = _attn(proj, _attn_bias(attn_rel_bias[layer]), row(attn_out_gain[layer]),
                       batch=batch, seq=seq)
        kv = _norm_matmul(mem2, row(norm_mem_kv[layer]), w_mem_kv[layer].astype(BF16),
                          tm=min(ROWS_MEM_KV, batch * mem_len), col_chunk=PROJ_COLS, name="mem_kv")
        h = _post(h, y_hgrn, y_attn, w_out[layer].astype(BF16), row(norm_mem_q[layer]),
                  w_mem_q[layer].astype(BF16), kv, w_mem_o[layer].astype(BF16),
                  seq=seq, mem_len=mem_len, tm=ROWS_POST)
        j = layer // 2
        if layer % 2 == 0:
            h = _ffn(h, row(norm_ffn[layer]), w_ffn_gate[j].astype(BF16), w_ffn_up[j].astype(BF16),
                     w_ffn_down[j].astype(BF16), tm=ROWS_FFN, tf=FFN_COLS)
        else:
            last = layer == depth - 1
            h = _moe(h, row(norm_ffn[layer]), w_router[j], w_exp_gate, w_exp_up, w_exp_down, j, row(norm_final),
                     final_norm=last, tm_route=ROWS_ROUTER, tm_rows=ROWS_COMBINE)
    if depth % 2 == 1:
        h = _norm_only(h, row(norm_final))
    return h.reshape(batch, seq, d)


def _norm_only_kernel(x_ref, g_ref, o_ref):
    o_ref[...] = _rms(x_ref[...], g_ref[...])


def _norm_only(x, gain, tm=1024):
    n, d = x.shape
    return pl.pallas_call(
        _norm_only_kernel,
        grid=(n // tm,),
        in_specs=[pl.BlockSpec((tm, d), lambda i: (i, 0)), pl.BlockSpec((1, d), lambda i: (0, 0))],
        out_specs=pl.BlockSpec((tm, d), lambda i: (i, 0)),
        out_shape=jax.ShapeDtypeStruct((n, d), F32),
        compiler_params=_params(("parallel",)),
        name="final_norm",
    )(x, gain)
```

```python
import functools

import numpy as np
import jax
import jax.numpy as jnp
from jax import lax
from jax.experimental import pallas as pl
from jax.experimental.pallas import tpu as pltpu

F32 = jnp.float32
BF16 = jnp.bfloat16

EPS = 1e-6
CHUNK = 64
LEFT_CHUNKS = 8
MAX_REL = 128
HGRN_HEADS = 4
HGRN_HEAD_DIM = 128
HGRN_WIDTH = HGRN_HEADS * HGRN_HEAD_DIM
ATTN_HEADS = 8
ATTN_HEAD_DIM = 64
ATTN_WIDTH = ATTN_HEADS * ATTN_HEAD_DIM
MEM_HEADS = 4
N_EXPERTS = 8
TOP_K = 2

SUB = 16
FAST = 32
FAST_MAX_DECAY = 60.0
ATTN_QBLK = 4 * CHUNK
ATTN_WIN = ATTN_QBLK + LEFT_CHUNKS * CHUNK
MOE_STEPS = 7
LANE_TILES = 8
MOE_ROWS = 144 * MOE_STEPS
NEG = -1e30
LOG2E = 1.4426950408889634

VMEM_LIMIT = 56 * 1024 * 1024

ROWS_IN_PROJ = 1024
ROWS_HGRN = 512
ROWS_POST = 1024
ROWS_MEM_KV = 512
ROWS_FFN = 1024
ROWS_ROUTER = 512
ROWS_COMBINE = 512
PROJ_COLS = 512
FFN_COLS = 256


def _params(sem):
    return pltpu.CompilerParams(dimension_semantics=sem, vmem_limit_bytes=VMEM_LIMIT)


def _rms(x, gain):
    ms = jnp.mean(x * x, axis=-1, keepdims=True)
    return x * lax.rsqrt(ms + EPS) * gain


def _softmax_rows(s):
    m = jnp.max(s, axis=-1, keepdims=True)
    e = jnp.exp(s - m)
    return e, jnp.sum(e, axis=-1, keepdims=True)


def _norm_matmul_kernel(x_ref, g_ref, w_ref, o_ref, *, col_chunk):
    xn = _rms(x_ref[...], g_ref[...]).astype(BF16)
    n_out = o_ref.shape[1]
    for c in range(n_out // col_chunk):
        sl = slice(c * col_chunk, (c + 1) * col_chunk)
        o_ref[:, sl] = jnp.dot(xn, w_ref[:, sl], preferred_element_type=F32).astype(o_ref.dtype)


def _norm_matmul(x, gain, w, *, tm, col_chunk, name):
    n, d = x.shape
    n_out = w.shape[1]
    return pl.pallas_call(
        functools.partial(_norm_matmul_kernel, col_chunk=col_chunk),
        grid=(n // tm,),
        in_specs=[
            pl.BlockSpec((tm, d), lambda i: (i, 0)),
            pl.BlockSpec((1, d), lambda i: (0, 0)),
            pl.BlockSpec((d, n_out), lambda i: (0, 0)),
        ],
        out_specs=pl.BlockSpec((tm, n_out), lambda i: (i, 0)),
        out_shape=jax.ShapeDtypeStruct((n, n_out), BF16),
        compiler_params=_params(("parallel",)),
        name=name,
    )(x, gain, w)


def _hgrn_exact_tile(q_ref, i_ref, st_ref, lf_scr, k_scr, bl_scr, kc_scr, vc_scr, o_scr, ts):
    hd = HGRN_HEAD_DIM
    r = lax.broadcasted_iota(jnp.int32, (SUB, SUB), 0)
    c = lax.broadcasted_iota(jnp.int32, (SUB, SUB), 1)
    ltri = jnp.where(r >= c, 1.0, 0.0).astype(BF16)
    rowid = lax.broadcasted_iota(jnp.int32, (SUB, hd), 0)

    def sub_chunk(j, carry):
        r0 = pl.multiple_of(j * SUB, SUB)
        lf = lf_scr[pl.ds(r0, SUB), :]
        hi = lf.astype(BF16)
        lo = (lf - hi.astype(F32)).astype(BF16)
        bl = (jnp.dot(ltri, hi, preferred_element_type=F32)
              + jnp.dot(ltri, lo, preferred_element_type=F32))
        bl_scr[...] = bl
        kc_scr[...] = k_scr[pl.ds(r0, SUB), :]
        vc_scr[...] = i_ref[pl.ds(r0, SUB), :].astype(F32)
        qf = q_ref[pl.ds(r0, SUB), :].astype(F32)
        for h in range(HGRN_HEADS):
            sl = slice(h * hd, (h + 1) * hd)
            blh = bl[:, sl]
            qh = qf[:, sl]
            st = st_ref[h]
            o = lax.dot_general((qh * jnp.exp(blh)).astype(BF16), st.astype(BF16),
                                (((1,), (1,)), ((), ())), preferred_element_type=F32)
            for s in range(SUB):
                bs = bl_scr[s:s + 1, sl]
                ks = kc_scr[s:s + 1, sl]
                vs = vc_scr[s:s + 1, sl]
                e = jnp.exp(jnp.minimum(blh - bs, 0.0))
                a = jnp.sum(qh * e * ks, axis=-1, keepdims=True)
                a = jnp.where(rowid >= s, a, 0.0)
                o = o + a * vs
            o_scr[pl.ds(r0, SUB), sl] = o
            bend = bl_scr[SUB - 1:SUB, sl]
            kdec = (kc_scr[:, sl] * jnp.exp(bend - blh)).astype(BF16)
            vt = vc_scr[:, sl].T.astype(BF16)
            st_ref[h] = st * jnp.exp(bend) + jnp.dot(vt, kdec, preferred_element_type=F32)
        return carry

    lax.fori_loop(0, ts // SUB, sub_chunk, 0)


def _hgrn_fast_tile(q_ref, i_ref, st_ref, k_scr, b_scr, o_scr, ts):
    hd = HGRN_HEAD_DIM
    r = lax.broadcasted_iota(jnp.int32, (CHUNK, CHUNK), 0)
    c = lax.broadcasted_iota(jnp.int32, (CHUNK, CHUNK), 1)
    causal = r >= c
    first_half = lax.broadcasted_iota(jnp.int32, (CHUNK, HGRN_WIDTH), 0) < FAST

    def block(j, carry):
        r0 = pl.multiple_of(j * CHUNK, CHUNK)
        b = b_scr[pl.ds(r0, CHUNK), :]
        b_mid = b[FAST - 1:FAST, :]
        rel = b - jnp.where(first_half, b_mid, 0.0)
        e_mid = jnp.exp(b_mid)
        e_tail = jnp.exp(b[CHUNK - 1:CHUNK, :])
        kdf = k_scr[pl.ds(r0, CHUNK), :] * jnp.exp(-rel)
        qd = (q_ref[pl.ds(r0, CHUNK), :].astype(F32) * jnp.exp(rel)).astype(BF16)
        kd = kdf.astype(BF16)
        v = i_ref[pl.ds(r0, CHUNK), :]
        kdec = (kdf * e_tail).astype(BF16)
        vt = v.astype(F32).T.astype(BF16)
        for h in range(HGRN_HEADS):
            sl = slice(h * hd, (h + 1) * hd)
            st = st_ref[h]
            a = lax.dot_general(qd[:, sl], kd[:, sl], (((1,), (1,)), ((), ())),
                                preferred_element_type=F32)
            a = jnp.where(causal, a, 0.0).astype(BF16)
            st_mid = (st * e_mid[:, sl]).astype(BF16)
            o_scr[pl.ds(r0, CHUNK), sl] = (
                jnp.dot(a, v[:, sl], preferred_element_type=F32)
                + lax.dot_general(qd[:, sl], st_mid, (((1,), (1,)), ((), ())),
                                  preferred_element_type=F32))
            st_ref[h] = (st * (e_mid[:, sl] * e_tail[:, sl])
                         + jnp.dot(vt[sl, :], kdec[:, sl], preferred_element_type=F32))
        return carry

    lax.fori_loop(0, ts // CHUNK, block, 0, unroll=True)


def _hgrn_kernel(q_ref, f_ref, i_ref, g_ref, lb_ref, gain_ref, tri_ref, o_ref,
                 st_ref, lf_scr, k_scr, b_scr, bl_scr, kc_scr, vc_scr, o_scr, *, ts):
    hd = HGRN_HEAD_DIM

    @pl.when(pl.program_id(1) == 0)
    def _():
        st_ref[...] = jnp.zeros_like(st_ref)

    lb = lb_ref[...]
    f = lb + (1.0 - lb) * jax.nn.sigmoid(f_ref[...].astype(F32))
    lf = jnp.log(f)
    lf_scr[...] = lf
    k_scr[...] = 1.0 - f
    hi = lf.astype(BF16)
    lo = (lf - hi.astype(F32)).astype(BF16)
    nblk = ts // CHUNK
    side = lambda x: jnp.concatenate([x[j * CHUNK:(j + 1) * CHUNK, :] for j in range(nblk)], axis=1)
    b = (jnp.dot(tri_ref[...], side(hi), preferred_element_type=F32)
         + jnp.dot(tri_ref[...], side(lo), preferred_element_type=F32))
    w = lf.shape[1]
    for j in range(nblk):
        b_scr[j * CHUNK:(j + 1) * CHUNK, :] = b[:, j * w:(j + 1) * w]
    mild = jnp.min(b) >= -FAST_MAX_DECAY

    @pl.when(mild)
    def _():
        _hgrn_fast_tile(q_ref, i_ref, st_ref, k_scr, b_scr, o_scr, ts)

    @pl.when(jnp.logical_not(mild))
    def _():
        _hgrn_exact_tile(q_ref, i_ref, st_ref, lf_scr, k_scr, bl_scr, kc_scr, vc_scr, o_scr, ts)

    gate = jax.nn.silu(g_ref[...].astype(F32)) * gain_ref[...]
    for h in range(HGRN_HEADS):
        sl = slice(h * hd, (h + 1) * hd)
        o = o_scr[:, sl]
        o = o * lax.rsqrt(jnp.mean(o * o, axis=-1, keepdims=True) + EPS)
        o_ref[:, sl] = (o * gate[:, sl]).astype(o_ref.dtype)


def _hgrn(proj, lb, gain, *, batch, seq, ts):
    n = batch * seq
    w = HGRN_WIDTH
    steps = seq // ts
    rows = np.arange(CHUNK)
    tri = jnp.asarray((rows[:, None] // FAST == rows[None, :] // FAST) & (rows[:, None] >= rows[None, :]), BF16)

    def col(j):
        return pl.BlockSpec((ts, w), lambda b, i, j=j: (b * steps + i, j))

    vec = pl.BlockSpec((1, w), lambda b, i: (0, 0))
    return pl.pallas_call(
        functools.partial(_hgrn_kernel, ts=ts),
        grid=(batch, steps),
        in_specs=[col(0), col(1), col(2), col(3), vec, vec, pl.BlockSpec((CHUNK, CHUNK), lambda b, i: (0, 0))],
        out_specs=pl.BlockSpec((ts, w), lambda b, i: (b * steps + i, 0)),
        out_shape=jax.ShapeDtypeStruct((n, w), BF16),
        scratch_shapes=[
            pltpu.VMEM((HGRN_HEADS, HGRN_HEAD_DIM, HGRN_HEAD_DIM), F32),
            pltpu.VMEM((ts, w), F32),
            pltpu.VMEM((ts, w), F32),
            pltpu.VMEM((ts, w), F32),
            pltpu.VMEM((SUB, w), F32),
            pltpu.VMEM((SUB, w), F32),
            pltpu.VMEM((SUB, w), F32),
            pltpu.VMEM((ts, w), F32),
        ],
        compiler_params=_params(("parallel", "arbitrary")),
        name="hgrn",
    )(proj, proj, proj, proj, lb, gain, tri)


def _attn_block(q, kwin, vwin, bias_ref, lo, gain, o_ref):
    qs = (q.astype(F32) * (ATTN_HEAD_DIM ** -0.5 * LOG2E)).astype(BF16)
    lane = lax.broadcasted_iota(jnp.int32, (ATTN_QBLK, 128), 1)
    first = lane < ATTN_HEAD_DIM
    ones = jnp.ones((kwin.shape[0], 128), BF16)
    outs = []
    for p in range(ATTN_HEADS // 2):
        sl = slice(p * 128, (p + 1) * 128)
        qp, kp, vp = qs[:, sl], kwin[:, sl], vwin[:, sl]
        zero = jnp.zeros_like(qp)
        q2 = jnp.concatenate([jnp.where(first, qp, zero), jnp.where(first, zero, qp)], axis=0)
        s = lax.dot_general(q2, kp, (((1,), (1,)), ((), ())), preferred_element_type=F32)
        s = s + bias_ref[p, :, lo:]
        e = jnp.exp2(s - jnp.max(s, axis=-1, keepdims=True)).astype(BF16)
        oa = jnp.dot(e, jnp.concatenate([vp, ones], axis=1), preferred_element_type=F32)
        o2 = oa[:, :128] / oa[:, 128:]
        outs.append(jnp.where(first, o2[:ATTN_QBLK], o2[ATTN_QBLK:]))
    o = jnp.concatenate(outs, axis=1)
    o_ref[...] = _rms(o, gain).astype(o_ref.dtype)


def _attn_kernel(q_ref, k_ref, v_ref, bias_ref, gain_ref, o_ref):
    i = pl.program_id(1)
    gain = gain_ref[...]
    left = LEFT_CHUNKS * CHUNK

    for blk in range(left // ATTN_QBLK):
        @pl.when(i == blk)
        def _(blk=blk):
            w = (blk + 1) * ATTN_QBLK
            _attn_block(q_ref[...], k_ref[0:w, :], v_ref[0:w, :], bias_ref, ATTN_WIN - w, gain, o_ref)

    @pl.when(i >= left // ATTN_QBLK)
    def _():
        start = pl.multiple_of(i * ATTN_QBLK - left, ATTN_QBLK)
        _attn_block(q_ref[...], k_ref[pl.ds(start, ATTN_WIN), :], v_ref[pl.ds(start, ATTN_WIN), :],
                    bias_ref, 0, gain, o_ref)


def _attn_bias(rel_bias):
    q, w = ATTN_QBLK, ATTN_WIN
    period = q + w - 1
    x = np.arange(period)
    x = np.where(x < w, x, x - period)
    vec = rel_bias.astype(F32)[:, np.clip(LEFT_CHUNKS * CHUNK - x, -MAX_REL, MAX_REL) + MAX_REL]
    bias = jnp.tile(vec, (1, q))[:, :q * (period - 1)].reshape(-1, q, period - 1)[:, :, :w]
    qi = np.arange(q)[:, None]
    kj = np.arange(w)[None, :]
    dchunk = qi // CHUNK + LEFT_CHUNKS - kj // CHUNK
    in_band = (dchunk >= 0) & (dchunk <= LEFT_CHUNKS)
    bias = jnp.where(in_band[None], bias * LOG2E, NEG)
    return bias.reshape(-1, 2 * q, w)


def _attn(proj, bias, gain, *, batch, seq):
    n = batch * seq
    w = ATTN_WIDTH
    steps = seq // ATTN_QBLK
    qcol = 4 * HGRN_WIDTH // w
    return pl.pallas_call(
        _attn_kernel,
        grid=(batch, steps),
        in_specs=[
            pl.BlockSpec((ATTN_QBLK, w), lambda b, i: (b * steps + i, qcol)),
            pl.BlockSpec((seq, w), lambda b, i: (b, qcol + 1)),
            pl.BlockSpec((seq, w), lambda b, i: (b, qcol + 2)),
            pl.BlockSpec((ATTN_HEADS // 2, 2 * ATTN_QBLK, ATTN_WIN), lambda b, i: (0, 0, 0)),
            pl.BlockSpec((1, w), lambda b, i: (0, 0)),
        ],
        out_specs=pl.BlockSpec((ATTN_QBLK, w), lambda b, i: (b * steps + i, 0)),
        out_shape=jax.ShapeDtypeStruct((n, w), BF16),
        compiler_params=_params(("parallel", "arbitrary")),
        name="attn",
    )(proj, proj, proj, bias, gain)


def _post_kernel(h_ref, yh_ref, ya_ref, wout_ref, gq_ref, wq_ref, kv_ref, wo_ref, o_ref):
    d = h_ref.shape[1]
    hw = yh_ref.shape[1]
    h1 = (h_ref[...]
          + jnp.dot(yh_ref[...], wout_ref[0:hw, :], preferred_element_type=F32)
          + jnp.dot(ya_ref[...], wout_ref[hw:, :], preferred_element_type=F32))
    hn = _rms(h1, gq_ref[...]).astype(BF16)
    dh = d // MEM_HEADS
    q = (jnp.dot(hn, wq_ref[...], preferred_element_type=F32) * (dh ** -0.5)).astype(BF16)
    outs = []
    for hd in range(MEM_HEADS):
        sl = slice(hd * dh, (hd + 1) * dh)
        k = kv_ref[:, sl]
        v = kv_ref[:, d + hd * dh:d + (hd + 1) * dh]
        s = lax.dot_general(q[:, sl], k, (((1,), (1,)), ((), ())), preferred_element_type=F32)
        e, l = _softmax_rows(s)
        outs.append((jnp.dot(e.astype(BF16), v, preferred_element_type=F32) / l).astype(BF16))
    o = jnp.concatenate(outs, axis=1)
    o_ref[...] = h1 + jnp.dot(o, wo_ref[...], preferred_element_type=F32)


def _post(h, yh, ya, w_out, gq, wq, kv, wo, *, seq, mem_len, tm):
    n, d = h.shape
    per_batch = seq // tm
    const = lambda i: (0, 0)
    resident = dict(pipeline_mode=pl.Buffered(1))
    return pl.pallas_call(
        _post_kernel,
        grid=(n // tm,),
        in_specs=[
            pl.BlockSpec((tm, d), lambda i: (i, 0)),
            pl.BlockSpec((tm, yh.shape[1]), lambda i: (i, 0)),
            pl.BlockSpec((tm, ya.shape[1]), lambda i: (i, 0)),
            pl.BlockSpec(w_out.shape, const, **resident),
            pl.BlockSpec((1, d), const),
            pl.BlockSpec(wq.shape, const, **resident),
            pl.BlockSpec((mem_len, 2 * d), lambda i: (i // per_batch, 0)),
            pl.BlockSpec(wo.shape, const, **resident),
        ],
        out_specs=pl.BlockSpec((tm, d), lambda i: (i, 0)),
        out_shape=jax.ShapeDtypeStruct((n, d), F32),
        compiler_params=_params(("parallel",)),
        name="post",
    )(h, yh, ya, w_out, gq, wq, kv, wo)


def _ffn_kernel(h_ref, g_ref, wg_ref, wu_ref, wd_ref, o_ref, xn_scr, acc_scr, *, tf):
    xn_scr[...] = _rms(h_ref[...], g_ref[...]).astype(BF16)
    acc_scr[...] = jnp.zeros_like(acc_scr)

    def slab(c, carry):
        c0 = pl.multiple_of(c * tf, tf)
        xn = xn_scr[...]
        a = jnp.dot(xn, wg_ref[:, pl.ds(c0, tf)], preferred_element_type=F32)
        u = jnp.dot(xn, wu_ref[:, pl.ds(c0, tf)], preferred_element_type=F32)
        act = (jax.nn.silu(a) * u).astype(BF16)
        acc_scr[...] += jnp.dot(act, wd_ref[pl.ds(c0, tf), :], preferred_element_type=F32)
        return carry

    lax.fori_loop(0, wg_ref.shape[1] // tf, slab, 0)
    o_ref[...] = h_ref[...] + acc_scr[...]


def _ffn(h, gain, wg, wu, wd, *, tm, tf):
    n, d = h.shape
    dff = wg.shape[1]
    resident = dict(pipeline_mode=pl.Buffered(1))
    return pl.pallas_call(
        functools.partial(_ffn_kernel, tf=tf),
        grid=(n // tm,),
        in_specs=[
            pl.BlockSpec((tm, d), lambda i: (i, 0)),
            pl.BlockSpec((1, d), lambda i: (0, 0)),
            pl.BlockSpec((d, dff), lambda i: (0, 0), **resident),
            pl.BlockSpec((d, dff), lambda i: (0, 0), **resident),
            pl.BlockSpec((dff, d), lambda i: (0, 0), **resident),
        ],
        out_specs=pl.BlockSpec((tm, d), lambda i: (i, 0)),
        out_shape=jax.ShapeDtypeStruct((n, d), F32),
        scratch_shapes=[pltpu.VMEM((tm, d), BF16), pltpu.VMEM((tm, d), F32)],
        compiler_params=_params(("parallel",)),
        name="ffn",
    )(h, gain, wg, wu, wd)


def _router_kernel(h_ref, g_ref, w1_ref, w2_ref, tri_ref, ri_ref, rg_ref, cnt_ref, hn_ref, carry_scr):
    i = pl.program_id(0)
    tm = h_ref.shape[0]

    @pl.when(i == 0)
    def _():
        carry_scr[...] = jnp.zeros_like(carry_scr)

    hn = _rms(h_ref[...], g_ref[...])
    for c in range(LANE_TILES):
        hn_ref[pl.ds(c, tm, stride=LANE_TILES), :] = hn[:, c * 128:(c + 1) * 128]
    a1 = hn.astype(BF16)
    a2 = (hn - a1.astype(F32)).astype(BF16)
    lg = (jnp.dot(a1, w1_ref[...], preferred_element_type=F32)
          + jnp.dot(a1, w2_ref[...], preferred_element_type=F32)
          + jnp.dot(a2, w1_ref[...], preferred_element_type=F32))
    lgt = lg.T[0:N_EXPERTS, :]
    ids = lax.broadcasted_iota(jnp.int32, (N_EXPERTS, tm), 0)
    m1 = jnp.max(lgt, axis=0, keepdims=True)
    i1 = jnp.min(jnp.where(lgt == m1, ids, N_EXPERTS), axis=0, keepdims=True)
    rest = jnp.where(ids == i1, -jnp.inf, lgt)
    m2 = jnp.max(rest, axis=0, keepdims=True)
    i2 = jnp.min(jnp.where(rest == m2, ids, N_EXPERTS), axis=0, keepdims=True)
    e = jnp.exp(m2 - m1)
    g1 = 1.0 / (1.0 + e)
    g2 = e * g1
    hit1 = ids == i1
    hit2 = ids == i2
    onehot = jnp.where(hit1 | hit2, 1.0, 0.0)
    before = jnp.dot(onehot.astype(BF16), tri_ref[...], preferred_element_type=F32) + carry_scr[...][:, 0:1]
    r1 = jnp.sum(jnp.where(hit1, before, 0.0), axis=0, keepdims=True)
    r2 = jnp.sum(jnp.where(hit2, before, 0.0), axis=0, keepdims=True)
    carry_scr[...] = carry_scr[...] + jnp.sum(onehot, axis=1, keepdims=True)
    cnt_ref[...] = carry_scr[...]
    row = lax.broadcasted_iota(jnp.int32, (8, tm), 0)
    ri_ref[...] = jnp.where(row == 0, i1, jnp.where(row == 1, i2, jnp.where(
        row == 2, r1.astype(jnp.int32), jnp.where(row == 3, r2.astype(jnp.int32), 0))))
    rg_ref[...] = jnp.where(row == 0, g1, jnp.where(row == 1, g2, 0.0))


def _router(h, gain, w1, w2, tri, *, tm):
    n, d = h.shape
    return pl.pallas_call(
        _router_kernel,
        grid=(n // tm,),
        in_specs=[
            pl.BlockSpec((tm, d), lambda i: (i, 0)),
            pl.BlockSpec((1, d), lambda i: (0, 0)),
            pl.BlockSpec(w1.shape, lambda i: (0, 0)),
            pl.BlockSpec(w2.shape, lambda i: (0, 0)),
            pl.BlockSpec((tm, tm), lambda i: (0, 0)),
        ],
        out_specs=[
            pl.BlockSpec((8, tm), lambda i: (0, i)),
            pl.BlockSpec((8, tm), lambda i: (0, i)),
            pl.BlockSpec((N_EXPERTS, 128), lambda i: (0, 0)),
            pl.BlockSpec((tm * LANE_TILES, 128), lambda i: (i, 0)),
        ],
        out_shape=[
            jax.ShapeDtypeStruct((8, n), jnp.int32),
            jax.ShapeDtypeStruct((8, n), F32),
            jax.ShapeDtypeStruct((N_EXPERTS, 128), F32),
            jax.ShapeDtypeStruct((n * LANE_TILES, 128), F32),
        ],
        scratch_shapes=[pltpu.VMEM((N_EXPERTS, 128), F32)],
        compiler_params=_params(("arbitrary",)),
        name="router",
    )(h, gain, w1, w2, tri)


def _experts_kernel(be_ref, nu_ref, idx_hbm, h_hbm, wg_hbm, wu_hbm, wd_hbm, y_hbm,
                    idx_smem, xbuf, obuf, xb_scr, acc_scr, wgb, wub, wdb, isem, gsem, ssem, wsem,
                    *, n_tokens, li):
    j = pl.program_id(0)
    nu = nu_ref[0]
    rows_per_step = MOE_ROWS // MOE_STEPS
    tf = wgb.shape[2]

    def idx_copy(b):
        s = lax.rem(b + 1, 4)
        return pltpu.make_async_copy(idx_hbm.at[b + 1], idx_smem.at[s], isem.at[s])

    def gather(b, step, u):
        tok = idx_smem[lax.rem(b + 1, 4), step, u]
        s = lax.rem(b + 2, 2)
        r = step * rows_per_step + u
        return pltpu.make_async_copy(h_hbm.at[pl.ds(tok * LANE_TILES, LANE_TILES)],
                                     xbuf.at[s, pl.ds(r * LANE_TILES, LANE_TILES)], gsem.at[s])

    def scatter(b, step, u):
        tgt = idx_smem[lax.rem(b + 1, 4), MOE_STEPS + step, u]
        s = lax.rem(b + 2, 2)
        r = step * rows_per_step + u
        return pltpu.make_async_copy(obuf.at[s, pl.ds(r * LANE_TILES, LANE_TILES)],
                                     y_hbm.at[pl.ds(tgt * LANE_TILES, LANE_TILES)], ssem.at[s])

    def wait_gathers(b):
        s = lax.rem(b + 2, 2)
        pltpu.make_async_copy(h_hbm.at[pl.ds(0, MOE_ROWS * LANE_TILES)], xbuf.at[s], gsem.at[s]).wait()

    def wait_scatters(b):
        s = lax.rem(b + 2, 2)
        pltpu.make_async_copy(obuf.at[s], y_hbm.at[pl.ds(0, MOE_ROWS * LANE_TILES)], ssem.at[s]).wait()

    def weight_copies(b, f):
        e = be_ref[b]
        s = lax.rem(b * MOE_STEPS + f, 2)
        c0 = pl.multiple_of(f * tf, tf)
        return (pltpu.make_async_copy(wg_hbm.at[li, e, :, pl.ds(c0, tf)], wgb.at[s], wsem.at[s]),
                pltpu.make_async_copy(wu_hbm.at[li, e, :, pl.ds(c0, tf)], wub.at[s], wsem.at[s]),
                pltpu.make_async_copy(wd_hbm.at[li, e, pl.ds(c0, tf), :], wdb.at[s], wsem.at[s]))

    def for_rows(fn, steps=range(MOE_STEPS)):
        for step in steps:
            def body(u, carry, step=step):
                fn(step, u)
                return carry
            lax.fori_loop(0, rows_per_step, body, 0, unroll=16)

    @pl.when(j == 0)
    def _():
        obuf[1] = jnp.zeros(obuf.shape[1:], obuf.dtype)
        for half in range(2):
            spare = pltpu.make_async_copy(
                obuf.at[1],
                y_hbm.at[pl.ds((TOP_K * n_tokens + half * MOE_ROWS) * LANE_TILES, MOE_ROWS * LANE_TILES)],
                ssem.at[half])
            spare.start()
            spare.wait()
        for b in (-1, 0, 1):
            idx_copy(b).start()
        idx_copy(-1).wait()
        idx_copy(0).wait()
        for c in weight_copies(0, 0):
            c.start()
        for_rows(lambda step, u: gather(0, step, u).start())

    @pl.when(j <= nu)
    def _():
        @pl.when(j == nu)
        def _():
            wait_gathers(j)

        @pl.when(j >= 1)
        def _():
            wait_scatters(j - 2)

        idx_copy(j + 1).wait()

        @pl.when(j < nu)
        def _():
            idx_copy(j + 2).start()

    @pl.when(j < nu)
    def _():
        slot = lax.rem(j, 2)
        acc_scr[...] = jnp.zeros_like(acc_scr)
        wait_gathers(j)
        for c in range(LANE_TILES):
            xb_scr[:, c * 128:(c + 1) * 128] = (
                xbuf[slot, pl.ds(c, MOE_ROWS, stride=LANE_TILES), :].astype(BF16))

        def slab(f, carry):
            for c in weight_copies(j, f):
                c.wait()
            last = f == MOE_STEPS - 1
            nb = jnp.where(last, j + 1, j)
            nf_ = jnp.where(last, 0, f + 1)

            @pl.when(nb < nu)
            def _():
                for c in weight_copies(nb, nf_):
                    c.start()

            ws = lax.rem(j * MOE_STEPS + f, 2)
            for u in range(rows_per_step):
                gather(j + 1, f, u).start()
                scatter(j - 1, f, u).start(priority=u % 2)
            xb = xb_scr[...]
            a = jnp.dot(xb, wgb[ws].astype(BF16), preferred_element_type=F32)
            u_ = jnp.dot(xb, wub[ws].astype(BF16), preferred_element_type=F32)
            act = (jax.nn.silu(a) * u_).astype(BF16)
            acc_scr[...] += jnp.dot(act, wdb[ws].astype(BF16), preferred_element_type=F32)
            return carry

        lax.fori_loop(0, MOE_STEPS, slab, 0)
        for c in range(LANE_TILES):
            obuf[slot, pl.ds(c, MOE_ROWS, stride=LANE_TILES), :] = acc_scr[:, c * 128:(c + 1) * 128]

    @pl.when(j == nu)
    def _():
        for_rows(lambda step, u: scatter(j - 1, step, u).start())
        wait_scatters(j - 1)


def _experts(block_expert, n_used, idx, h3, wg, wu, wd, li):
    n = h3.shape[0] // LANE_TILES
    d = LANE_TILES * 128
    dff = wg.shape[3]
    tf = dff // MOE_STEPS
    n_blocks = idx.shape[0] - 3
    hbm = pl.BlockSpec(memory_space=pl.ANY)
    grid_spec = pltpu.PrefetchScalarGridSpec(
        num_scalar_prefetch=2,
        grid=(n_blocks + 1,),
        in_specs=[hbm, hbm, hbm, hbm, hbm],
        out_specs=hbm,
        scratch_shapes=[
            pltpu.SMEM((4, 2 * MOE_STEPS, MOE_ROWS // MOE_STEPS), jnp.int32),
            pltpu.VMEM((2, MOE_ROWS * LANE_TILES, 128), F32),
            pltpu.VMEM((2, MOE_ROWS * LANE_TILES, 128), F32),
            pltpu.VMEM((MOE_ROWS, d), BF16),
            pltpu.VMEM((MOE_ROWS, d), F32),
            pltpu.VMEM((2, d, tf), F32),
            pltpu.VMEM((2, d, tf), F32),
            pltpu.VMEM((2, tf, d), F32),
            pltpu.SemaphoreType.DMA((4,)),
            pltpu.SemaphoreType.DMA((2,)),
            pltpu.SemaphoreType.DMA((2,)),
            pltpu.SemaphoreType.DMA((2,)),
        ],
    )
    return pl.pallas_call(
        functools.partial(_experts_kernel, n_tokens=n, li=li),
        grid_spec=grid_spec,
        out_shape=jax.ShapeDtypeStruct(((TOP_K * n + 2 * MOE_ROWS) * LANE_TILES, 128), F32),
        compiler_params=_params(("arbitrary",)),
        name="experts",
    )(block_expert, n_used, idx, h3, wg, wu, wd)


def _combine_kernel(h_ref, rg_ref, gf_ref, y0_ref, y1_ref, o_ref, *, final_norm):
    gates = rg_ref[...].T
    g0, g1 = gates[:, 0:1], gates[:, 1:2]
    tm = h_ref.shape[0]
    out = jnp.concatenate([g0 * y0_ref[pl.ds(c, tm, stride=LANE_TILES), :]
                           + g1 * y1_ref[pl.ds(c, tm, stride=LANE_TILES), :]
                           for c in range(LANE_TILES)], axis=1)
    out = h_ref[...] + out
    if final_norm:
        out = _rms(out, gf_ref[...])
    o_ref[...] = out


def _combine(h, rg, gain_final, y, *, tm, final_norm):
    n, d = h.shape
    return pl.pallas_call(
        functools.partial(_combine_kernel, final_norm=final_norm),
        grid=(n // tm,),
        in_specs=[
            pl.BlockSpec((tm, d), lambda i: (i, 0)),
            pl.BlockSpec((8, tm), lambda i: (0, i)),
            pl.BlockSpec((1, d), lambda i: (0, 0)),
            pl.BlockSpec((tm * LANE_TILES, 128), lambda i: (i, 0)),
            pl.BlockSpec((tm * LANE_TILES, 128), lambda i: (i + n // tm, 0)),
        ],
        out_specs=pl.BlockSpec((tm, d), lambda i: (i, 0)),
        out_shape=jax.ShapeDtypeStruct((n, d), F32),
        compiler_params=_params(("parallel",)),
        name="combine",
    )(h, rg, gain_final, y, y)


def _moe(h, gain, w_router, wg, wu, wd, li, gain_final, *, final_norm, tm_route, tm_rows):
    n, d = h.shape
    wr = jnp.zeros((d, 128), F32).at[:, :N_EXPERTS].set(w_router.astype(F32))
    w1 = wr.astype(BF16)
    w2 = (wr - w1.astype(F32)).astype(BF16)
    tri = jnp.asarray(np.triu(np.ones((tm_route, tm_route), np.float32), 1), BF16)
    ri, rg, cnt, hn3 = _router(h, gain, w1, w2, tri, tm=tm_route)

    counts = cnt[:, 0].astype(jnp.int32)
    padded = (counts + MOE_ROWS - 1) // MOE_ROWS * MOE_ROWS
    padded_end = jnp.cumsum(padded)
    padded_start = padded_end - padded
    n_blocks = -(-(n * TOP_K + N_EXPERTS * (MOE_ROWS - 1)) // MOE_ROWS)
    block_start = jnp.arange(n_blocks, dtype=jnp.int32) * MOE_ROWS
    block_expert = jnp.minimum(
        jnp.sum(block_start[:, None] >= padded_end[None, :], axis=1), N_EXPERTS - 1).astype(jnp.int32)
    n_used = (padded_end[-1:] // MOE_ROWS).astype(jnp.int32)
    onehot = ri[0:TOP_K, :, None] == jnp.arange(N_EXPERTS, dtype=jnp.int32)
    dest = jnp.sum(jnp.where(onehot, padded_start, 0), axis=-1) + ri[TOP_K:2 * TOP_K]

    n_idx = (n_blocks + 3) * MOE_ROWS
    pos = jnp.arange(n_idx, dtype=jnp.int32)
    spare = TOP_K * n + (pos // MOE_ROWS % 2) * MOE_ROWS + pos % MOE_ROWS
    hit = jnp.zeros((n_idx,), jnp.int32).at[dest.reshape(-1) + MOE_ROWS].add(
        jnp.arange(1, TOP_K * n + 1, dtype=jnp.int32), unique_indices=True)
    target = jnp.where(hit > 0, hit - 1, spare)
    token = jnp.where(target < TOP_K * n, target % n, 0)
    per_step = MOE_ROWS // MOE_STEPS
    idx = jnp.concatenate([token.reshape(-1, MOE_STEPS, per_step),
                           target.reshape(-1, MOE_STEPS, per_step)], axis=1)

    y = _experts(block_expert, n_used, idx, hn3, wg, wu, wd, li)
    return _combine(h, rg, gain_final, y, tm=tm_rows, final_norm=final_norm)


def kernel(x, mem, w_in, hgrn_lower_bound, hgrn_out_gain, attn_rel_bias, attn_out_gain, w_out,
           norm_mix, norm_mem_q, norm_mem_kv, w_mem_q, w_mem_kv, w_mem_o, norm_ffn,
           w_ffn_gate, w_ffn_up, w_ffn_down, w_router, w_exp_gate, w_exp_up, w_exp_down, norm_final):
    batch, seq, d = x.shape
    mem_len = mem.shape[1]
    depth = w_in.shape[0]
    n = batch * seq
    assert seq % ROWS_POST == 0 and seq >= ATTN_WIN and n % max(ROWS_IN_PROJ, ROWS_FFN) == 0

    lb_probs = jax.nn.softmax(hgrn_lower_bound.astype(F32), axis=0)
    lower_bounds = jnp.cumsum(lb_probs, axis=0) - lb_probs[0]

    h = x.reshape(n, d)
    mem2 = mem.reshape(batch * mem_len, d)
    row = lambda v: v.reshape(1, -1).astype(F32)
    for layer in range(depth):
        proj = _norm_matmul(h, row(norm_mix[layer]), w_in[layer].astype(BF16),
                            tm=ROWS_IN_PROJ, col_chunk=PROJ_COLS, name="in_proj")
        y_hgrn = _hgrn(proj, row(lower_bounds[layer]), row(hgrn_out_gain[layer]),
                       batch=batch, seq=seq, ts=ROWS_HGRN)
        y_attn = _attn(proj, _attn_bias(attn_rel_bias[layer]), row(attn_out_gain[layer]),
                       batch=batch, seq=seq)
        kv = _norm_matmul(mem2, row(norm_mem_kv[layer]), w_mem_kv[layer].astype(BF16),
                          tm=min(ROWS_MEM_KV, batch * mem_len), col_chunk=PROJ_COLS, name="mem_kv")
        h = _post(h, y_hgrn, y_attn, w_out[layer].astype(BF16), row(norm_mem_q[layer]),
                  w_mem_q[layer].astype(BF16), kv, w_mem_o[layer].astype(BF16),
                  seq=seq, mem_len=mem_len, tm=ROWS_POST)
        j = layer // 2
        if layer % 2 == 0:
            h = _ffn(h, row(norm_ffn[layer]), w_ffn_gate[j].astype(BF16), w_ffn_up[j].astype(BF16),
                     w_ffn_down[j].astype(BF16), tm=ROWS_FFN, tf=FFN_COLS)
        else:
            last = layer == depth - 1
            h = _moe(h, row(norm_ffn[layer]), w_router[j], w_exp_gate, w_exp_up, w_exp_down, j, row(norm_final),
                     final_norm=last, tm_route=ROWS_ROUTER, tm_rows=ROWS_COMBINE)
    if depth % 2 == 1:
        h = _norm_only(h, row(norm_final))
    return h.reshape(batch, seq, d)


def _norm_only_kernel(x_ref, g_ref, o_ref):
    o_ref[...] = _rms(x_ref[...], g_ref[...])


def _norm_only(x, gain, tm=1024):
    n, d = x.shape
    return pl.pallas_call(
        _norm_only_kernel,
        grid=(n // tm,),
        in_specs=[pl.BlockSpec((tm, d), lambda i: (i, 0)), pl.BlockSpec((1, d), lambda i: (0, 0))],
        out_specs=pl.BlockSpec((tm, d), lambda i: (i, 0)),
        out_shape=jax.ShapeDtypeStruct((n, d), F32),
        compiler_params=_params(("parallel",)),
        name="final_norm",
    )(x, gain)
```

```python
import functools

import numpy as np
import jax
import jax.numpy as jnp
from jax import lax
from jax.experimental import pallas as pl
from jax.experimental.pallas import tpu as pltpu

F32 = jnp.float32
BF16 = jnp.bfloat16

EPS = 1e-6
CHUNK = 64
LEFT_CHUNKS = 8
MAX_REL = 128
HGRN_HEADS = 4
HGRN_HEAD_DIM = 128
HGRN_WIDTH = HGRN_HEADS * HGRN_HEAD_DIM
ATTN_HEADS = 8
ATTN_HEAD_DIM = 64
ATTN_WIDTH = ATTN_HEADS * ATTN_HEAD_DIM
MEM_HEADS = 4
N_EXPERTS = 8
TOP_K = 2

SUB = 16
FAST = 32
FAST_MAX_DECAY = 60.0
HGRN_PAIR = 2
ATTN_QBLK = 4 * CHUNK
ATTN_WIN = ATTN_QBLK + LEFT_CHUNKS * CHUNK
ATTN_BLOCKS_PER_STEP = 2
MOE_STEPS = 7
LANE_TILES = 8
MOE_ROWS = 144 * MOE_STEPS
NEG = -1e30
LOG2E = 1.4426950408889634

VMEM_LIMIT = 56 * 1024 * 1024

ROWS_IN_PROJ = 1024
ROWS_HGRN = 512
ROWS_POST = 1024
ROWS_MEM_KV = 512
ROWS_FFN = 1024
ROWS_ROUTER = 1024
ROWS_COMBINE = 1024
PROJ_COLS = 512
FFN_COLS = 256


def _params(sem):
    return pltpu.CompilerParams(dimension_semantics=sem, vmem_limit_bytes=VMEM_LIMIT)


def _rms(x, gain):
    ms = jnp.mean(x * x, axis=-1, keepdims=True)
    return x * lax.rsqrt(ms + EPS) * gain


def _softmax_rows(s):
    m = jnp.max(s, axis=-1, keepdims=True)
    e = jnp.exp(s - m)
    return e, jnp.sum(e, axis=-1, keepdims=True)


def _norm_matmul_kernel(x_ref, g_ref, w_ref, o_ref, *, col_chunk):
    xn = _rms(x_ref[...], g_ref[...]).astype(BF16)
    n_out = o_ref.shape[1]
    for c in range(n_out // col_chunk):
        sl = slice(c * col_chunk, (c + 1) * col_chunk)
        o_ref[:, sl] = jnp.dot(xn, w_ref[:, sl], preferred_element_type=F32).astype(o_ref.dtype)


def _norm_matmul(x, gain, w, *, tm, col_chunk, name):
    n, d = x.shape
    n_out = w.shape[1]
    return pl.pallas_call(
        functools.partial(_norm_matmul_kernel, col_chunk=col_chunk),
        grid=(n // tm,),
        in_specs=[
            pl.BlockSpec((tm, d), lambda i: (i, 0)),
            pl.BlockSpec((1, d), lambda i: (0, 0)),
            pl.BlockSpec((d, n_out), lambda i: (0, 0)),
        ],
        out_specs=pl.BlockSpec((tm, n_out), lambda i: (i, 0)),
        out_shape=jax.ShapeDtypeStruct((n, n_out), BF16),
        compiler_params=_params(("parallel",)),
        name=name,
    )(x, gain, w)


def _hgrn_exact_tile(q_ref, i_ref, st_ref, lf_scr, k_scr, bl_scr, kc_scr, vc_scr, o_scr, ts):
    hd = HGRN_HEAD_DIM
    r = lax.broadcasted_iota(jnp.int32, (SUB, SUB), 0)
    c = lax.broadcasted_iota(jnp.int32, (SUB, SUB), 1)
    ltri = jnp.where(r >= c, 1.0, 0.0).astype(BF16)
    rowid = lax.broadcasted_iota(jnp.int32, (SUB, hd), 0)

    def sub_chunk(j, carry):
        r0 = pl.multiple_of(j * SUB, SUB)
        lf = lf_scr[pl.ds(r0, SUB), :]
        hi = lf.astype(BF16)
        lo = (lf - hi.astype(F32)).astype(BF16)
        bl = (jnp.dot(ltri, hi, preferred_element_type=F32)
              + jnp.dot(ltri, lo, preferred_element_type=F32))
        bl_scr[...] = bl
        kc_scr[...] = k_scr[pl.ds(r0, SUB), :]
        vc_scr[...] = i_ref[pl.ds(r0, SUB), :].astype(F32)
        qf = q_ref[pl.ds(r0, SUB), :].astype(F32)
        for h in range(HGRN_HEADS):
            sl = slice(h * hd, (h + 1) * hd)
            blh = bl[:, sl]
            qh = qf[:, sl]
            st = st_ref[h]
            o = lax.dot_general((qh * jnp.exp(blh)).astype(BF16), st.astype(BF16),
                                (((1,), (1,)), ((), ())), preferred_element_type=F32)
            for s in range(SUB):
                bs = bl_scr[s:s + 1, sl]
                ks = kc_scr[s:s + 1, sl]
                vs = vc_scr[s:s + 1, sl]
                e = jnp.exp(jnp.minimum(blh - bs, 0.0))
                a = jnp.sum(qh * e * ks, axis=-1, keepdims=True)
                a = jnp.where(rowid >= s, a, 0.0)
                o = o + a * vs
            o_scr[pl.ds(r0, SUB), sl] = o
            bend = bl_scr[SUB - 1:SUB, sl]
            kdec = (kc_scr[:, sl] * jnp.exp(bend - blh)).astype(BF16)
            vt = vc_scr[:, sl].T.astype(BF16)
            st_ref[h] = st * jnp.exp(bend) + jnp.dot(vt, kdec, preferred_element_type=F32)
        return carry

    lax.fori_loop(0, ts // SUB, sub_chunk, 0)


def _hgrn_fast_tile(q_ref, i_ref, st_ref, k_scr, b_scr, o_scr, ts):
    hd = HGRN_HEAD_DIM
    r = lax.broadcasted_iota(jnp.int32, (CHUNK, CHUNK), 0)
    c = lax.broadcasted_iota(jnp.int32, (CHUNK, CHUNK), 1)
    causal = r >= c
    first_half = lax.broadcasted_iota(jnp.int32, (CHUNK, HGRN_WIDTH), 0) < FAST

    def block(j, carry):
        r0 = pl.multiple_of(j * CHUNK, CHUNK)
        b = b_scr[pl.ds(r0, CHUNK), :]
        b_mid = b[FAST - 1:FAST, :]
        rel = b - jnp.where(first_half, b_mid, 0.0)
        e_mid = jnp.exp(b_mid)
        e_tail = jnp.exp(b[CHUNK - 1:CHUNK, :])
        kdf = k_scr[pl.ds(r0, CHUNK), :] * jnp.exp(-rel)
        qd = (q_ref[pl.ds(r0, CHUNK), :].astype(F32) * jnp.exp(rel)).astype(BF16)
        kd = kdf.astype(BF16)
        v = i_ref[pl.ds(r0, CHUNK), :]
        kdec = (kdf * e_tail).astype(BF16)
        vt = v.astype(F32).T.astype(BF16)
        for h in range(HGRN_HEADS):
            sl = slice(h * hd, (h + 1) * hd)
            st = st_ref[h]
            a = lax.dot_general(qd[:, sl], kd[:, sl], (((1,), (1,)), ((), ())),
                                preferred_element_type=F32)
            a = jnp.where(causal, a, 0.0).astype(BF16)
            st_mid = (st * e_mid[:, sl]).astype(BF16)
            o_scr[pl.ds(r0, CHUNK), sl] = (
                jnp.dot(a, v[:, sl], preferred_element_type=F32)
                + lax.dot_general(qd[:, sl], st_mid, (((1,), (1,)), ((), ())),
                                  preferred_element_type=F32))
            st_ref[h] = (st * (e_mid[:, sl] * e_tail[:, sl])
                         + jnp.dot(vt[sl, :], kdec[:, sl], preferred_element_type=F32))
        return carry

    lax.fori_loop(0, ts // CHUNK, block, 0, unroll=True)


def _hgrn_kernel(q_ref, f_ref, i_ref, g_ref, lb_ref, gain_ref, tri_ref, o_ref,
                 st_ref, lf_scr, k_scr, b_scr, bl_scr, kc_scr, vc_scr, o_scr, *, ts):
    hd = HGRN_HEAD_DIM

    @pl.when(pl.program_id(1) == 0)
    def _():
        st_ref[...] = jnp.zeros_like(st_ref)

    lb = lb_ref[...]
    nblk = ts // CHUNK
    side = lambda x: jnp.concatenate([x[j * CHUNK:(j + 1) * CHUNK, :] for j in range(nblk)], axis=1)
    mild = None
    for p in range(HGRN_PAIR):
        f = lb + (1.0 - lb) * jax.nn.sigmoid(f_ref[p].astype(F32))
        lf = jnp.log(f)
        lf_scr[p] = lf
        k_scr[p] = 1.0 - f
        hi = lf.astype(BF16)
        lo = (lf - hi.astype(F32)).astype(BF16)
        b = (jnp.dot(tri_ref[...], side(hi), preferred_element_type=F32)
             + jnp.dot(tri_ref[...], side(lo), preferred_element_type=F32))
        w = lf.shape[1]
        for j in range(nblk):
            b_scr[p, j * CHUNK:(j + 1) * CHUNK, :] = b[:, j * w:(j + 1) * w]
        ok = jnp.min(b) >= -FAST_MAX_DECAY
        mild = ok if mild is None else jnp.logical_and(mild, ok)

    @pl.when(mild)
    def _():
        for p in range(HGRN_PAIR):
            _hgrn_fast_tile(q_ref.at[p], i_ref.at[p], st_ref.at[p], k_scr.at[p], b_scr.at[p], o_scr.at[p], ts)

    @pl.when(jnp.logical_not(mild))
    def _():
        def one(p, carry):
            _hgrn_exact_tile(q_ref.at[p], i_ref.at[p], st_ref.at[p], lf_scr.at[p], k_scr.at[p],
                             bl_scr, kc_scr, vc_scr, o_scr.at[p], ts)
            return carry
        lax.fori_loop(0, HGRN_PAIR, one, 0)

    for p in range(HGRN_PAIR):
        gate = jax.nn.silu(g_ref[p].astype(F32)) * gain_ref[...]
        for h in range(HGRN_HEADS):
            sl = slice(h * hd, (h + 1) * hd)
            o = o_scr[p, :, sl]
            o = o * lax.rsqrt(jnp.mean(o * o, axis=-1, keepdims=True) + EPS)
            o_ref[p, :, sl] = (o * gate[:, sl]).astype(o_ref.dtype)


def _hgrn(proj, lb, gain, *, batch, seq, ts):
    n = batch * seq
    w = HGRN_WIDTH
    steps = seq // ts
    pair = HGRN_PAIR
    rows = np.arange(CHUNK)
    tri = jnp.asarray((rows[:, None] // FAST == rows[None, :] // FAST) & (rows[:, None] >= rows[None, :]), BF16)
    proj4 = proj.reshape(batch // pair, pair, seq, proj.shape[1])

    def col(j):
        return pl.BlockSpec((None, pair, ts, w), lambda b, i, j=j: (b, 0, i, j))

    vec = pl.BlockSpec((1, w), lambda b, i: (0, 0))
    out = pl.pallas_call(
        functools.partial(_hgrn_kernel, ts=ts),
        grid=(batch // pair, steps),
        in_specs=[col(0), col(1), col(2), col(3), vec, vec, pl.BlockSpec((CHUNK, CHUNK), lambda b, i: (0, 0))],
        out_specs=pl.BlockSpec((None, pair, ts, w), lambda b, i: (b, 0, i, 0)),
        out_shape=jax.ShapeDtypeStruct((batch // pair, pair, seq, w), BF16),
        scratch_shapes=[
            pltpu.VMEM((pair, HGRN_HEADS, HGRN_HEAD_DIM, HGRN_HEAD_DIM), F32),
            pltpu.VMEM((pair, ts, w), F32),
            pltpu.VMEM((pair, ts, w), F32),
            pltpu.VMEM((pair, ts, w), F32),
            pltpu.VMEM((SUB, w), F32),
            pltpu.VMEM((SUB, w), F32),
            pltpu.VMEM((SUB, w), F32),
            pltpu.VMEM((pair, ts, w), F32),
        ],
        compiler_params=_params(("parallel", "arbitrary")),
        name="hgrn",
    )(proj4, proj4, proj4, proj4, lb, gain, tri)
    return out.reshape(n, w)


def _attn_block(q, kwin, vwin, bias_ref, lo, gain, o_ref):
    qs = (q.astype(F32) * (ATTN_HEAD_DIM ** -0.5 * LOG2E)).astype(BF16)
    lane = lax.broadcasted_iota(jnp.int32, (ATTN_QBLK, 128), 1)
    first = lane < ATTN_HEAD_DIM
    ones = jnp.ones((kwin.shape[0], 128), BF16)
    outs = []
    for p in range(ATTN_HEADS // 2):
        sl = slice(p * 128, (p + 1) * 128)
        qp, kp, vp = qs[:, sl], kwin[:, sl], vwin[:, sl]
        zero = jnp.zeros_like(qp)
        q2 = jnp.concatenate([jnp.where(first, qp, zero), jnp.where(first, zero, qp)], axis=0)
        s = lax.dot_general(q2, kp, (((1,), (1,)), ((), ())), preferred_element_type=F32)
        s = s + bias_ref[p, :, lo:]
        e = jnp.exp2(s - jnp.max(s, axis=-1, keepdims=True)).astype(BF16)
        oa = jnp.dot(e, jnp.concatenate([vp, ones], axis=1), preferred_element_type=F32)
        o2 = oa[:, :128] / oa[:, 128:]
        outs.append(jnp.where(first, o2[:ATTN_QBLK], o2[ATTN_QBLK:]))
    o = jnp.concatenate(outs, axis=1)
    o_ref[...] = _rms(o, gain).astype(o_ref.dtype)


def _attn_kernel(q_ref, k_ref, v_ref, bias_ref, gain_ref, o_ref):
    i = pl.program_id(1)
    gain = gain_ref[...]
    left = LEFT_CHUNKS * CHUNK
    short = left // ATTN_QBLK

    def rows(s):
        return pl.ds(s * ATTN_QBLK, ATTN_QBLK)

    @pl.when(i == 0)
    def _():
        for blk in range(ATTN_BLOCKS_PER_STEP):
            if blk < short:
                w = (blk + 1) * ATTN_QBLK
                _attn_block(q_ref[rows(blk), :], k_ref[0:w, :], v_ref[0:w, :], bias_ref, ATTN_WIN - w, gain,
                            o_ref.at[rows(blk), :])
            else:
                start = blk * ATTN_QBLK - left
                _attn_block(q_ref[rows(blk), :], k_ref[start:start + ATTN_WIN, :], v_ref[start:start + ATTN_WIN, :],
                            bias_ref, 0, gain, o_ref.at[rows(blk), :])

    @pl.when(i > 0)
    def _():
        for s in range(ATTN_BLOCKS_PER_STEP):
            start = pl.multiple_of((i * ATTN_BLOCKS_PER_STEP + s) * ATTN_QBLK - left, ATTN_QBLK)
            _attn_block(q_ref[rows(s), :], k_ref[pl.ds(start, ATTN_WIN), :], v_ref[pl.ds(start, ATTN_WIN), :],
                        bias_ref, 0, gain, o_ref.at[rows(s), :])


def _attn_bias(rel_bias):
    q, w = ATTN_QBLK, ATTN_WIN
    period = q + w - 1
    x = np.arange(period)
    x = np.where(x < w, x, x - period)
    vec = rel_bias.astype(F32)[:, np.clip(LEFT_CHUNKS * CHUNK - x, -MAX_REL, MAX_REL) + MAX_REL]
    bias = jnp.tile(vec, (1, q))[:, :q * (period - 1)].reshape(-1, q, period - 1)[:, :, :w]
    qi = np.arange(q)[:, None]
    kj = np.arange(w)[None, :]
    dchunk = qi // CHUNK + LEFT_CHUNKS - kj // CHUNK
    in_band = (dchunk >= 0) & (dchunk <= LEFT_CHUNKS)
    bias = jnp.where(in_band[None], bias * LOG2E, NEG)
    return bias.reshape(-1, 2 * q, w)


def _attn(proj, bias, gain, *, batch, seq):
    n = batch * seq
    w = ATTN_WIDTH
    rows = ATTN_QBLK * ATTN_BLOCKS_PER_STEP
    steps = seq // rows
    qcol = 4 * HGRN_WIDTH // w
    return pl.pallas_call(
        _attn_kernel,
        grid=(batch, steps),
        in_specs=[
            pl.BlockSpec((rows, w), lambda b, i: (b * steps + i, qcol)),
            pl.BlockSpec((seq, w), lambda b, i: (b, qcol + 1)),
            pl.BlockSpec((seq, w), lambda b, i: (b, qcol + 2)),
            pl.BlockSpec((ATTN_HEADS // 2, 2 * ATTN_QBLK, ATTN_WIN), lambda b, i: (0, 0, 0)),
            pl.BlockSpec((1, w), lambda b, i: (0, 0)),
        ],
        out_specs=pl.BlockSpec((rows, w), lambda b, i: (b * steps + i, 0)),
        out_shape=jax.ShapeDtypeStruct((n, w), BF16),
        compiler_params=_params(("parallel", "arbitrary")),
        name="attn",
    )(proj, proj, proj, bias, gain)


def _post_kernel(h_ref, yh_ref, ya_ref, wout_ref, gq_ref, wq_ref, kv_ref, wo_ref, o_ref):
    d = h_ref.shape[1]
    hw = yh_ref.shape[1]
    h1 = (h_ref[...]
          + jnp.dot(yh_ref[...], wout_ref[0:hw, :], preferred_element_type=F32)
          + jnp.dot(ya_ref[...], wout_ref[hw:, :], preferred_element_type=F32))
    hn = _rms(h1, gq_ref[...]).astype(BF16)
    dh = d // MEM_HEADS
    q = (jnp.dot(hn, wq_ref[...], preferred_element_type=F32) * (dh ** -0.5)).astype(BF16)
    outs = []
    for hd in range(MEM_HEADS):
        sl = slice(hd * dh, (hd + 1) * dh)
        k = kv_ref[:, sl]
        v = kv_ref[:, d + hd * dh:d + (hd + 1) * dh]
        s = lax.dot_general(q[:, sl], k, (((1,), (1,)), ((), ())), preferred_element_type=F32)
        e, l = _softmax_rows(s)
        outs.append((jnp.dot(e.astype(BF16), v, preferred_element_type=F32) / l).astype(BF16))
    o = jnp.concatenate(outs, axis=1)
    o_ref[...] = h1 + jnp.dot(o, wo_ref[...], preferred_element_type=F32)


def _post(h, yh, ya, w_out, gq, wq, kv, wo, *, seq, mem_len, tm):
    n, d = h.shape
    per_batch = seq // tm
    const = lambda i: (0, 0)
    resident = dict(pipeline_mode=pl.Buffered(1))
    return pl.pallas_call(
        _post_kernel,
        grid=(n // tm,),
        in_specs=[
            pl.BlockSpec((tm, d), lambda i: (i, 0)),
            pl.BlockSpec((tm, yh.shape[1]), lambda i: (i, 0)),
            pl.BlockSpec((tm, ya.shape[1]), lambda i: (i, 0)),
            pl.BlockSpec(w_out.shape, const, **resident),
            pl.BlockSpec((1, d), const),
            pl.BlockSpec(wq.shape, const, **resident),
            pl.BlockSpec((mem_len, 2 * d), lambda i: (i // per_batch, 0)),
            pl.BlockSpec(wo.shape, const, **resident),
        ],
        out_specs=pl.BlockSpec((tm, d), lambda i: (i, 0)),
        out_shape=jax.ShapeDtypeStruct((n, d), F32),
        compiler_params=_params(("parallel",)),
        name="post",
    )(h, yh, ya, w_out, gq, wq, kv, wo)


def _ffn_kernel(h_ref, g_ref, wg_ref, wu_ref, wd_ref, o_ref, xn_scr, acc_scr, *, tf):
    xn_scr[...] = _rms(h_ref[...], g_ref[...]).astype(BF16)
    acc_scr[...] = jnp.zeros_like(acc_scr)

    def slab(c, carry):
        c0 = pl.multiple_of(c * tf, tf)
        xn = xn_scr[...]
        a = jnp.dot(xn, wg_ref[:, pl.ds(c0, tf)], preferred_element_type=F32)
        u = jnp.dot(xn, wu_ref[:, pl.ds(c0, tf)], preferred_element_type=F32)
        act = (jax.nn.silu(a) * u).astype(BF16)
        acc_scr[...] += jnp.dot(act, wd_ref[pl.ds(c0, tf), :], preferred_element_type=F32)
        return carry

    lax.fori_loop(0, wg_ref.shape[1] // tf, slab, 0)
    o_ref[...] = h_ref[...] + acc_scr[...]


def _ffn(h, gain, wg, wu, wd, *, tm, tf):
    n, d = h.shape
    dff = wg.shape[1]
    resident = dict(pipeline_mode=pl.Buffered(1))
    return pl.pallas_call(
        functools.partial(_ffn_kernel, tf=tf),
        grid=(n // tm,),
        in_specs=[
            pl.BlockSpec((tm, d), lambda i: (i, 0)),
            pl.BlockSpec((1, d), lambda i: (0, 0)),
            pl.BlockSpec((d, dff), lambda i: (0, 0), **resident),
            pl.BlockSpec((d, dff), lambda i: (0, 0), **resident),
            pl.BlockSpec((dff, d), lambda i: (0, 0), **resident),
        ],
        out_specs=pl.BlockSpec((tm, d), lambda i: (i, 0)),
        out_shape=jax.ShapeDtypeStruct((n, d), F32),
        scratch_shapes=[pltpu.VMEM((tm, d), BF16), pltpu.VMEM((tm, d), F32)],
        compiler_params=_params(("parallel",)),
        name="ffn",
    )(h, gain, wg, wu, wd)


def _router_kernel(h_ref, g_ref, w1_ref, w2_ref, tri_ref, ri_ref, rg_ref, cnt_ref, hn_ref, carry_scr):
    i = pl.program_id(0)
    tm = h_ref.shape[0]

    @pl.when(i == 0)
    def _():
        carry_scr[...] = jnp.zeros_like(carry_scr)

    hn = _rms(h_ref[...], g_ref[...])
    for c in range(LANE_TILES):
        hn_ref[pl.ds(c, tm, stride=LANE_TILES), :] = hn[:, c * 128:(c + 1) * 128]
    a1 = hn.astype(BF16)
    a2 = (hn - a1.astype(F32)).astype(BF16)
    lg = (jnp.dot(a1, w1_ref[...], preferred_element_type=F32)
          + jnp.dot(a1, w2_ref[...], preferred_element_type=F32)
          + jnp.dot(a2, w1_ref[...], preferred_element_type=F32))
    lgt = lg.T[0:N_EXPERTS, :]
    ids = lax.broadcasted_iota(jnp.int32, (N_EXPERTS, tm), 0)
    m1 = jnp.max(lgt, axis=0, keepdims=True)
    i1 = jnp.min(jnp.where(lgt == m1, ids, N_EXPERTS), axis=0, keepdims=True)
    rest = jnp.where(ids == i1, -jnp.inf, lgt)
    m2 = jnp.max(rest, axis=0, keepdims=True)
    i2 = jnp.min(jnp.where(rest == m2, ids, N_EXPERTS), axis=0, keepdims=True)
    e = jnp.exp(m2 - m1)
    g1 = 1.0 / (1.0 + e)
    g2 = e * g1
    hit1 = ids == i1
    hit2 = ids == i2
    onehot = jnp.where(hit1 | hit2, 1.0, 0.0)
    before = jnp.dot(onehot.astype(BF16), tri_ref[...], preferred_element_type=F32) + carry_scr[...][:, 0:1]
    r1 = jnp.sum(jnp.where(hit1, before, 0.0), axis=0, keepdims=True)
    r2 = jnp.sum(jnp.where(hit2, before, 0.0), axis=0, keepdims=True)
    carry_scr[...] = carry_scr[...] + jnp.sum(onehot, axis=1, keepdims=True)
    cnt_ref[...] = carry_scr[...]
    row = lax.broadcasted_iota(jnp.int32, (8, tm), 0)
    ri_ref[...] = jnp.where(row == 0, i1, jnp.where(row == 1, i2, jnp.where(
        row == 2, r1.astype(jnp.int32), jnp.where(row == 3, r2.astype(jnp.int32), 0))))
    rg_ref[...] = jnp.where(row == 0, g1, jnp.where(row == 1, g2, 0.0))


def _router(h, gain, w1, w2, tri, *, tm):
    n, d = h.shape
    return pl.pallas_call(
        _router_kernel,
        grid=(n // tm,),
        in_specs=[
            pl.BlockSpec((tm, d), lambda i: (i, 0)),
            pl.BlockSpec((1, d), lambda i: (0, 0)),
            pl.BlockSpec(w1.shape, lambda i: (0, 0)),
            pl.BlockSpec(w2.shape, lambda i: (0, 0)),
            pl.BlockSpec((tm, tm), lambda i: (0, 0)),
        ],
        out_specs=[
            pl.BlockSpec((8, tm), lambda i: (0, i)),
            pl.BlockSpec((8, tm), lambda i: (0, i)),
            pl.BlockSpec((N_EXPERTS, 128), lambda i: (0, 0)),
            pl.BlockSpec((tm * LANE_TILES, 128), lambda i: (i, 0)),
        ],
        out_shape=[
            jax.ShapeDtypeStruct((8, n), jnp.int32),
            jax.ShapeDtypeStruct((8, n), F32),
            jax.ShapeDtypeStruct((N_EXPERTS, 128), F32),
            jax.ShapeDtypeStruct((n * LANE_TILES, 128), F32),
        ],
        scratch_shapes=[pltpu.VMEM((N_EXPERTS, 128), F32)],
        compiler_params=_params(("arbitrary",)),
        name="router",
    )(h, gain, w1, w2, tri)


def _experts_kernel(be_ref, nu_ref, idx_hbm, h_hbm, wg_hbm, wu_hbm, wd_hbm, y_hbm,
                    idx_smem, xbuf, obuf, xb_scr, acc_scr, wgb, wub, wdb, isem, gsem, ssem, wsem,
                    *, n_tokens, li):
    j = pl.program_id(0)
    nu = nu_ref[0]
    rows_per_step = MOE_ROWS // MOE_STEPS
    tf = wgb.shape[2]

    def idx_copy(b):
        s = lax.rem(b + 1, 4)
        return pltpu.make_async_copy(idx_hbm.at[b + 1], idx_smem.at[s], isem.at[s])

    def gather(b, step, u):
        tok = idx_smem[lax.rem(b + 1, 4), step, u]
        s = lax.rem(b + 2, 2)
        r = step * rows_per_step + u
        return pltpu.make_async_copy(h_hbm.at[pl.ds(tok * LANE_TILES, LANE_TILES)],
                                     xbuf.at[s, pl.ds(r * LANE_TILES, LANE_TILES)], gsem.at[s])

    def scatter(b, step, u):
        tgt = idx_smem[lax.rem(b + 1, 4), MOE_STEPS + step, u]
        s = lax.rem(b + 2, 2)
        r = step * rows_per_step + u
        return pltpu.make_async_copy(obuf.at[s, pl.ds(r * LANE_TILES, LANE_TILES)],
                                     y_hbm.at[pl.ds(tgt * LANE_TILES, LANE_TILES)], ssem.at[s])

    def wait_gathers(b):
        s = lax.rem(b + 2, 2)
        pltpu.make_async_copy(h_hbm.at[pl.ds(0, MOE_ROWS * LANE_TILES)], xbuf.at[s], gsem.at[s]).wait()

    def wait_scatters(b):
        s = lax.rem(b + 2, 2)
        pltpu.make_async_copy(obuf.at[s], y_hbm.at[pl.ds(0, MOE_ROWS * LANE_TILES)], ssem.at[s]).wait()

    def weight_copies(b, f):
        e = be_ref[b]
        s = lax.rem(b * MOE_STEPS + f, 2)
        c0 = pl.multiple_of(f * tf, tf)
        return (pltpu.make_async_copy(wg_hbm.at[li, e, :, pl.ds(c0, tf)], wgb.at[s], wsem.at[s]),
                pltpu.make_async_copy(wu_hbm.at[li, e, :, pl.ds(c0, tf)], wub.at[s], wsem.at[s]),
                pltpu.make_async_copy(wd_hbm.at[li, e, pl.ds(c0, tf), :], wdb.at[s], wsem.at[s]))

    def for_rows(fn, steps=range(MOE_STEPS)):
        for step in steps:
            def body(u, carry, step=step):
                fn(step, u)
                return carry
            lax.fori_loop(0, rows_per_step, body, 0, unroll=16)

    @pl.when(j == 0)
    def _():
        obuf[1] = jnp.zeros(obuf.shape[1:], obuf.dtype)
        for half in range(2):
            spare = pltpu.make_async_copy(
                obuf.at[1],
                y_hbm.at[pl.ds((TOP_K * n_tokens + half * MOE_ROWS) * LANE_TILES, MOE_ROWS * LANE_TILES)],
                ssem.at[half])
            spare.start()
            spare.wait()
        for b in (-1, 0, 1):
            idx_copy(b).start()
        idx_copy(-1).wait()
        idx_copy(0).wait()
        for c in weight_copies(0, 0):
            c.start()
        for_rows(lambda step, u: gather(0, step, u).start())

    @pl.when(j <= nu)
    def _():
        @pl.when(j == nu)
        def _():
            wait_gathers(j)

        @pl.when(j >= 1)
        def _():
            wait_scatters(j - 2)

        idx_copy(j + 1).wait()

        @pl.when(j < nu)
        def _():
            idx_copy(j + 2).start()

    @pl.when(j < nu)
    def _():
        slot = lax.rem(j, 2)
        acc_scr[...] = jnp.zeros_like(acc_scr)
        wait_gathers(j)
        for c in range(LANE_TILES):
            xb_scr[:, c * 128:(c + 1) * 128] = (
                xbuf[slot, pl.ds(c, MOE_ROWS, stride=LANE_TILES), :].astype(BF16))

        def slab(f, carry):
            for c in weight_copies(j, f):
                c.wait()
            last = f == MOE_STEPS - 1
            nb = jnp.where(last, j + 1, j)
            nf_ = jnp.where(last, 0, f + 1)

            @pl.when(nb < nu)
            def _():
                for c in weight_copies(nb, nf_):
                    c.start()

            ws = lax.rem(j * MOE_STEPS + f, 2)
            for u in range(rows_per_step):
                gather(j + 1, f, u).start()
                scatter(j - 1, f, u).start(priority=u % 2)
            xb = xb_scr[...]
            a = jnp.dot(xb, wgb[ws].astype(BF16), preferred_element_type=F32)
            u_ = jnp.dot(xb, wub[ws].astype(BF16), preferred_element_type=F32)
            act = (jax.nn.silu(a) * u_).astype(BF16)
            acc_scr[...] += jnp.dot(act, wdb[ws].astype(BF16), preferred_element_type=F32)
            return carry

        lax.fori_loop(0, MOE_STEPS, slab, 0)
        for c in range(LANE_TILES):
            obuf[slot, pl.ds(c, MOE_ROWS, stride=LANE_TILES), :] = acc_scr[:, c * 128:(c + 1) * 128]

    @pl.when(j == nu)
    def _():
        for_rows(lambda step, u: scatter(j - 1, step, u).start())
        wait_scatters(j - 1)


def _experts(block_expert, n_used, idx, h3, wg, wu, wd, li):
    n = h3.shape[0] // LANE_TILES
    d = LANE_TILES * 128
    dff = wg.shape[3]
    tf = dff // MOE_STEPS
    n_blocks = idx.shape[0] - 3
    hbm = pl.BlockSpec(memory_space=pl.ANY)
    grid_spec = pltpu.PrefetchScalarGridSpec(
        num_scalar_prefetch=2,
        grid=(n_blocks + 1,),
        in_specs=[hbm, hbm, hbm, hbm, hbm],
        out_specs=hbm,
        scratch_shapes=[
            pltpu.SMEM((4, 2 * MOE_STEPS, MOE_ROWS // MOE_STEPS), jnp.int32),
            pltpu.VMEM((2, MOE_ROWS * LANE_TILES, 128), F32),
            pltpu.VMEM((2, MOE_ROWS * LANE_TILES, 128), F32),
            pltpu.VMEM((MOE_ROWS, d), BF16),
            pltpu.VMEM((MOE_ROWS, d), F32),
            pltpu.VMEM((2, d, tf), F32),
            pltpu.VMEM((2, d, tf), F32),
            pltpu.VMEM((2, tf, d), F32),
            pltpu.SemaphoreType.DMA((4,)),
            pltpu.SemaphoreType.DMA((2,)),
            pltpu.SemaphoreType.DMA((2,)),
            pltpu.SemaphoreType.DMA((2,)),
        ],
    )
    return pl.pallas_call(
        functools.partial(_experts_kernel, n_tokens=n, li=li),
        grid_spec=grid_spec,
        out_shape=jax.ShapeDtypeStruct(((TOP_K * n + 2 * MOE_ROWS) * LANE_TILES, 128), F32),
        compiler_params=_params(("arbitrary",)),
        name="experts",
    )(block_expert, n_used, idx, h3, wg, wu, wd)


def _combine_kernel(h_ref, rg_ref, gf_ref, y0_ref, y1_ref, o_ref, *, final_norm):
    gates = rg_ref[...].T
    g0, g1 = gates[:, 0:1], gates[:, 1:2]
    tm = h_ref.shape[0]
    out = jnp.concatenate([g0 * y0_ref[pl.ds(c, tm, stride=LANE_TILES), :]
                           + g1 * y1_ref[pl.ds(c, tm, stride=LANE_TILES), :]
                           for c in range(LANE_TILES)], axis=1)
    out = h_ref[...] + out
    if final_norm:
        out = _rms(out, gf_ref[...])
    o_ref[...] = out


def _combine(h, rg, gain_final, y, *, tm, final_norm):
    n, d = h.shape
    return pl.pallas_call(
        functools.partial(_combine_kernel, final_norm=final_norm),
        grid=(n // tm,),
        in_specs=[
            pl.BlockSpec((tm, d), lambda i: (i, 0)),
            pl.BlockSpec((8, tm), lambda i: (0, i)),
            pl.BlockSpec((1, d), lambda i: (0, 0)),
            pl.BlockSpec((tm * LANE_TILES, 128), lambda i: (i, 0)),
            pl.BlockSpec((tm * LANE_TILES, 128), lambda i: (i + n // tm, 0)),
        ],
        out_specs=pl.BlockSpec((tm, d), lambda i: (i, 0)),
        out_shape=jax.ShapeDtypeStruct((n, d), F32),
        compiler_params=_params(("parallel",)),
        name="combine",
    )(h, rg, gain_final, y, y)


def _moe(h, gain, w_router, wg, wu, wd, li, gain_final, *, final_norm, tm_route, tm_rows):
    n, d = h.shape
    wr = jnp.zeros((d, 128), F32).at[:, :N_EXPERTS].set(w_router.astype(F32))
    w1 = wr.astype(BF16)
    w2 = (wr - w1.astype(F32)).astype(BF16)
    tri = jnp.asarray(np.triu(np.ones((tm_route, tm_route), np.float32), 1), BF16)
    ri, rg, cnt, hn3 = _router(h, gain, w1, w2, tri, tm=tm_route)

    counts = cnt[:, 0].astype(jnp.int32)
    padded = (counts + MOE_ROWS - 1) // MOE_ROWS * MOE_ROWS
    padded_end = jnp.cumsum(padded)
    padded_start = padded_end - padded
    n_blocks = -(-(n * TOP_K + N_EXPERTS * (MOE_ROWS - 1)) // MOE_ROWS)
    block_start = jnp.arange(n_blocks, dtype=jnp.int32) * MOE_ROWS
    block_expert = jnp.minimum(
        jnp.sum(block_start[:, None] >= padded_end[None, :], axis=1), N_EXPERTS - 1).astype(jnp.int32)
    n_used = (padded_end[-1:] // MOE_ROWS).astype(jnp.int32)
    onehot = ri[0:TOP_K, :, None] == jnp.arange(N_EXPERTS, dtype=jnp.int32)
    dest = jnp.sum(jnp.where(onehot, padded_start, 0), axis=-1) + ri[TOP_K:2 * TOP_K]

    n_idx = (n_blocks + 3) * MOE_ROWS
    pos = jnp.arange(n_idx, dtype=jnp.int32)
    spare = TOP_K * n + (pos // MOE_ROWS % 2) * MOE_ROWS + pos % MOE_ROWS
    hit = jnp.zeros((n_idx,), jnp.int32).at[dest.reshape(-1) + MOE_ROWS].add(
        jnp.arange(1, TOP_K * n + 1, dtype=jnp.int32), unique_indices=True)
    target = jnp.where(hit > 0, hit - 1, spare)
    token = jnp.where(target < TOP_K * n, target % n, 0)
    per_step = MOE_ROWS // MOE_STEPS
    idx = jnp.concatenate([token.reshape(-1, MOE_STEPS, per_step),
                           target.reshape(-1, MOE_STEPS, per_step)], axis=1)

    y = _experts(block_expert, n_used, idx, hn3, wg, wu, wd, li)
    return _combine(h, rg, gain_final, y, tm=tm_rows, final_norm=final_norm)


def kernel(x, mem, w_in, hgrn_lower_bound, hgrn_out_gain, attn_rel_bias, attn_out_gain, w_out,
           norm_mix, norm_mem_q, norm_mem_kv, w_mem_q, w_mem_kv, w_mem_o, norm_ffn,
           w_ffn_gate, w_ffn_up, w_ffn_down, w_router, w_exp_gate, w_exp_up, w_exp_down, norm_final):
    batch, seq, d = x.shape
    mem_len = mem.shape[1]
    depth = w_in.shape[0]
    n = batch * seq
    assert seq % ROWS_POST == 0 and seq >= ATTN_WIN and n % max(ROWS_IN_PROJ, ROWS_FFN) == 0
    assert batch % HGRN_PAIR == 0

    lb_probs = jax.nn.softmax(hgrn_lower_bound.astype(F32), axis=0)
    lower_bounds = jnp.cumsum(lb_probs, axis=0) - lb_probs[0]

    h = x.reshape(n, d)
    mem2 = mem.reshape(batch * mem_len, d)
    row = lambda v: v.reshape(1, -1).astype(F32)
    for layer in range(depth):
        proj = _norm_matmul(h, row(norm_mix[layer]), w_in[layer].astype(BF16),
                            tm=ROWS_IN_PROJ, col_chunk=PROJ_COLS, name="in_proj")
        y_hgrn = _hgrn(proj, row(lower_bounds[layer]), row(hgrn_out_gain[layer]),
                       batch=batch, seq=seq, ts=ROWS_HGRN)
        y_attn = _attn(proj, _attn_bias(attn_rel_bias[layer]), row(attn_out_gain[layer]),
                       batch=batch, seq=seq)
        kv = _norm_matmul(mem2, row(norm_mem_kv[layer]), w_mem_kv[layer].astype(BF16),
                          tm=min(ROWS_MEM_KV, batch * mem_len), col_chunk=PROJ_COLS, name="mem_kv")
        h = _post(h, y_hgrn, y_attn, w_out[layer].astype(BF16), row(norm_mem_q[layer]),
                  w_mem_q[layer].astype(BF16), kv, w_mem_o[layer].astype(BF16),
                  seq=seq, mem_len=mem_len, tm=ROWS_POST)
        j = layer // 2
        if layer % 2 == 0:
            h = _ffn(h, row(norm_ffn[layer]), w_ffn_gate[j].astype(BF16), w_ffn_up[j].astype(BF16),
                     w_ffn_down[j].astype(BF16), tm=ROWS_FFN, tf=FFN_COLS)
        else:
            last = layer == depth - 1
            h = _moe(h, row(norm_ffn[layer]), w_router[j], w_exp_gate, w_exp_up, w_exp_down, j, row(norm_final),
                     final_norm=last, tm_route=ROWS_ROUTER, tm_rows=ROWS_COMBINE)
    if depth % 2 == 1:
        h = _norm_only(h, row(norm_final))
    return h.reshape(batch, seq, d)


def _norm_only_kernel(x_ref, g_ref, o_ref):
    o_ref[...] = _rms(x_ref[...], g_ref[...])


def _norm_only(x, gain, tm=1024):
    n, d = x.shape
    return pl.pallas_call(
        _norm_only_kernel,
        grid=(n // tm,),
        in_specs=[pl.BlockSpec((tm, d), lambda i: (i, 0)), pl.BlockSpec((1, d), lambda i: (0, 0))],
        out_specs=pl.BlockSpec((tm, d), lambda i: (i, 0)),
        out_shape=jax.ShapeDtypeStruct((n, d), F32),
        compiler_params=_params(("parallel",)),
        name="final_norm",
    )(x, gain)
```

```python
import functools

import numpy as np
import jax
import jax.numpy as jnp
from jax import lax
from jax.experimental import pallas as pl
from jax.experimental.pallas import tpu as pltpu

F32 = jnp.float32
BF16 = jnp.bfloat16

EPS = 1e-6
CHUNK = 64
LEFT_CHUNKS = 8
MAX_REL = 128
HGRN_HEADS = 4
HGRN_HEAD_DIM = 128
HGRN_WIDTH = HGRN_HEADS * HGRN_HEAD_DIM
ATTN_HEADS = 8
ATTN_HEAD_DIM = 64
ATTN_WIDTH = ATTN_HEADS * ATTN_HEAD_DIM
MEM_HEADS = 4
N_EXPERTS = 8
TOP_K = 2

SUB = 16
FAST = 32
FAST_MAX_DECAY = 60.0
HGRN_PAIR = 2
ATTN_QBLK = 4 * CHUNK
ATTN_WIN = ATTN_QBLK + LEFT_CHUNKS * CHUNK
ATTN_BLOCKS_PER_STEP = 4
MOE_STEPS = 7
LANE_TILES = 8
MOE_ROWS = 144 * MOE_STEPS
NEG = -1e30
LOG2E = 1.4426950408889634

VMEM_LIMIT = 56 * 1024 * 1024

ROWS_IN_PROJ = 1024
ROWS_HGRN = 512
ROWS_POST = 1024
ROWS_MEM_KV = 512
ROWS_FFN = 1024
ROWS_ROUTER = 1024
ROWS_COMBINE = 1024
PROJ_COLS = 512
FFN_COLS = 256


def _params(sem):
    return pltpu.CompilerParams(dimension_semantics=sem, vmem_limit_bytes=VMEM_LIMIT)


def _rms(x, gain):
    ms = jnp.mean(x * x, axis=-1, keepdims=True)
    return x * lax.rsqrt(ms + EPS) * gain


def _softmax_rows(s):
    m = jnp.max(s, axis=-1, keepdims=True)
    e = jnp.exp(s - m)
    return e, jnp.sum(e, axis=-1, keepdims=True)


def _norm_matmul_kernel(x_ref, g_ref, w_ref, o_ref, *, col_chunk):
    xn = _rms(x_ref[...], g_ref[...]).astype(BF16)
    n_out = o_ref.shape[1]
    for c in range(n_out // col_chunk):
        sl = slice(c * col_chunk, (c + 1) * col_chunk)
        o_ref[:, sl] = jnp.dot(xn, w_ref[:, sl], preferred_element_type=F32).astype(o_ref.dtype)


def _norm_matmul(x, gain, w, *, tm, col_chunk, name):
    n, d = x.shape
    n_out = w.shape[1]
    return pl.pallas_call(
        functools.partial(_norm_matmul_kernel, col_chunk=col_chunk),
        grid=(n // tm,),
        in_specs=[
            pl.BlockSpec((tm, d), lambda i: (i, 0)),
            pl.BlockSpec((1, d), lambda i: (0, 0)),
            pl.BlockSpec((d, n_out), lambda i: (0, 0)),
        ],
        out_specs=pl.BlockSpec((tm, n_out), lambda i: (i, 0)),
        out_shape=jax.ShapeDtypeStruct((n, n_out), BF16),
        compiler_params=_params(("parallel",)),
        name=name,
    )(x, gain, w)


def _hgrn_exact_tile(q_ref, i_ref, st_ref, lf_scr, k_scr, bl_scr, kc_scr, vc_scr, o_scr, ts):
    hd = HGRN_HEAD_DIM
    r = lax.broadcasted_iota(jnp.int32, (SUB, SUB), 0)
    c = lax.broadcasted_iota(jnp.int32, (SUB, SUB), 1)
    ltri = jnp.where(r >= c, 1.0, 0.0).astype(BF16)
    rowid = lax.broadcasted_iota(jnp.int32, (SUB, hd), 0)

    def sub_chunk(j, carry):
        r0 = pl.multiple_of(j * SUB, SUB)
        lf = lf_scr[pl.ds(r0, SUB), :]
        hi = lf.astype(BF16)
        lo = (lf - hi.astype(F32)).astype(BF16)
        bl = (jnp.dot(ltri, hi, preferred_element_type=F32)
              + jnp.dot(ltri, lo, preferred_element_type=F32))
        bl_scr[...] = bl
        kc_scr[...] = k_scr[pl.ds(r0, SUB), :]
        vc_scr[...] = i_ref[pl.ds(r0, SUB), :].astype(F32)
        qf = q_ref[pl.ds(r0, SUB), :].astype(F32)
        for h in range(HGRN_HEADS):
            sl = slice(h * hd, (h + 1) * hd)
            blh = bl[:, sl]
            qh = qf[:, sl]
            st = st_ref[h]
            o = lax.dot_general((qh * jnp.exp(blh)).astype(BF16), st.astype(BF16),
                                (((1,), (1,)), ((), ())), preferred_element_type=F32)
            for s in range(SUB):
                bs = bl_scr[s:s + 1, sl]
                ks = kc_scr[s:s + 1, sl]
                vs = vc_scr[s:s + 1, sl]
                e = jnp.exp(jnp.minimum(blh - bs, 0.0))
                a = jnp.sum(qh * e * ks, axis=-1, keepdims=True)
                a = jnp.where(rowid >= s, a, 0.0)
                o = o + a * vs
            o_scr[pl.ds(r0, SUB), sl] = o
            bend = bl_scr[SUB - 1:SUB, sl]
            kdec = (kc_scr[:, sl] * jnp.exp(bend - blh)).astype(BF16)
            vt = vc_scr[:, sl].T.astype(BF16)
            st_ref[h] = st * jnp.exp(bend) + jnp.dot(vt, kdec, preferred_element_type=F32)
        return carry

    lax.fori_loop(0, ts // SUB, sub_chunk, 0)


def _hgrn_fast_tile(q_ref, i_ref, st_ref, k_scr, b_scr, o_scr, ts):
    hd = HGRN_HEAD_DIM
    r = lax.broadcasted_iota(jnp.int32, (CHUNK, CHUNK), 0)
    c = lax.broadcasted_iota(jnp.int32, (CHUNK, CHUNK), 1)
    causal = r >= c
    first_half = lax.broadcasted_iota(jnp.int32, (CHUNK, HGRN_WIDTH), 0) < FAST

    def block(j, carry):
        r0 = pl.multiple_of(j * CHUNK, CHUNK)
        b = b_scr[pl.ds(r0, CHUNK), :]
        b_mid = b[FAST - 1:FAST, :]
        rel = b - jnp.where(first_half, b_mid, 0.0)
        e_mid = jnp.exp(b_mid)
        e_tail = jnp.exp(b[CHUNK - 1:CHUNK, :])
        kdf = k_scr[pl.ds(r0, CHUNK), :] * jnp.exp(-rel)
        qd = (q_ref[pl.ds(r0, CHUNK), :].astype(F32) * jnp.exp(rel)).astype(BF16)
        kd = kdf.astype(BF16)
        v = i_ref[pl.ds(r0, CHUNK), :]
        kdec = (kdf * e_tail).astype(BF16)
        vt = v.astype(F32).T.astype(BF16)
        for h in range(HGRN_HEADS):
            sl = slice(h * hd, (h + 1) * hd)
            st = st_ref[h]
            a = lax.dot_general(qd[:, sl], kd[:, sl], (((1,), (1,)), ((), ())),
                                preferred_element_type=F32)
            a = jnp.where(causal, a, 0.0).astype(BF16)
            st_mid = (st * e_mid[:, sl]).astype(BF16)
            o_scr[pl.ds(r0, CHUNK), sl] = (
                jnp.dot(a, v[:, sl], preferred_element_type=F32)
                + lax.dot_general(qd[:, sl], st_mid, (((1,), (1,)), ((), ())),
                                  preferred_element_type=F32))
            st_ref[h] = (st * (e_mid[:, sl] * e_tail[:, sl])
                         + jnp.dot(vt[sl, :], kdec[:, sl], preferred_element_type=F32))
        return carry

    lax.fori_loop(0, ts // CHUNK, block, 0, unroll=True)


def _hgrn_kernel(q_ref, f_ref, i_ref, g_ref, lb_ref, gain_ref, tri_ref, o_ref,
                 st_ref, lf_scr, k_scr, b_scr, bl_scr, kc_scr, vc_scr, o_scr, *, ts):
    hd = HGRN_HEAD_DIM

    @pl.when(pl.program_id(1) == 0)
    def _():
        st_ref[...] = jnp.zeros_like(st_ref)

    lb = lb_ref[...]
    nblk = ts // CHUNK
    side = lambda x: jnp.concatenate([x[j * CHUNK:(j + 1) * CHUNK, :] for j in range(nblk)], axis=1)
    mild = None
    for p in range(HGRN_PAIR):
        f = lb + (1.0 - lb) * jax.nn.sigmoid(f_ref[p].astype(F32))
        lf = jnp.log(f)
        lf_scr[p] = lf
        k_scr[p] = 1.0 - f
        hi = lf.astype(BF16)
        lo = (lf - hi.astype(F32)).astype(BF16)
        b = (jnp.dot(tri_ref[...], side(hi), preferred_element_type=F32)
             + jnp.dot(tri_ref[...], side(lo), preferred_element_type=F32))
        w = lf.shape[1]
        for j in range(nblk):
            b_scr[p, j * CHUNK:(j + 1) * CHUNK, :] = b[:, j * w:(j + 1) * w]
        ok = jnp.min(b) >= -FAST_MAX_DECAY
        mild = ok if mild is None else jnp.logical_and(mild, ok)

    @pl.when(mild)
    def _():
        for p in range(HGRN_PAIR):
            _hgrn_fast_tile(q_ref.at[p], i_ref.at[p], st_ref.at[p], k_scr.at[p], b_scr.at[p], o_scr.at[p], ts)

    @pl.when(jnp.logical_not(mild))
    def _():
        def one(p, carry):
            _hgrn_exact_tile(q_ref.at[p], i_ref.at[p], st_ref.at[p], lf_scr.at[p], k_scr.at[p],
                             bl_scr, kc_scr, vc_scr, o_scr.at[p], ts)
            return carry
        lax.fori_loop(0, HGRN_PAIR, one, 0)

    for p in range(HGRN_PAIR):
        gate = jax.nn.silu(g_ref[p].astype(F32)) * gain_ref[...]
        for h in range(HGRN_HEADS):
            sl = slice(h * hd, (h + 1) * hd)
            o = o_scr[p, :, sl]
            o = o * lax.rsqrt(jnp.mean(o * o, axis=-1, keepdims=True) + EPS)
            o_ref[p, :, sl] = (o * gate[:, sl]).astype(o_ref.dtype)


def _hgrn(proj, lb, gain, *, batch, seq, ts):
    n = batch * seq
    w = HGRN_WIDTH
    steps = seq // ts
    pair = HGRN_PAIR
    rows = np.arange(CHUNK)
    tri = jnp.asarray((rows[:, None] // FAST == rows[None, :] // FAST) & (rows[:, None] >= rows[None, :]), BF16)
    proj4 = proj.reshape(batch // pair, pair, seq, proj.shape[1])

    def col(j):
        return pl.BlockSpec((None, pair, ts, w), lambda b, i, j=j: (b, 0, i, j))

    vec = pl.BlockSpec((1, w), lambda b, i: (0, 0))
    out = pl.pallas_call(
        functools.partial(_hgrn_kernel, ts=ts),
        grid=(batch // pair, steps),
        in_specs=[col(0), col(1), col(2), col(3), vec, vec, pl.BlockSpec((CHUNK, CHUNK), lambda b, i: (0, 0))],
        out_specs=pl.BlockSpec((None, pair, ts, w), lambda b, i: (b, 0, i, 0)),
        out_shape=jax.ShapeDtypeStruct((batch // pair, pair, seq, w), BF16),
        scratch_shapes=[
            pltpu.VMEM((pair, HGRN_HEADS, HGRN_HEAD_DIM, HGRN_HEAD_DIM), F32),
            pltpu.VMEM((pair, ts, w), F32),
            pltpu.VMEM((pair, ts, w), F32),
            pltpu.VMEM((pair, ts, w), F32),
            pltpu.VMEM((SUB, w), F32),
            pltpu.VMEM((SUB, w), F32),
            pltpu.VMEM((SUB, w), F32),
            pltpu.VMEM((pair, ts, w), F32),
        ],
        compiler_params=_params(("parallel", "arbitrary")),
        name="hgrn",
    )(proj4, proj4, proj4, proj4, lb, gain, tri)
    return out.reshape(n, w)


def _attn_block(q, kwin, vwin, bias_ref, lo, gain, o_ref):
    qs = (q.astype(F32) * (ATTN_HEAD_DIM ** -0.5 * LOG2E)).astype(BF16)
    lane = lax.broadcasted_iota(jnp.int32, (ATTN_QBLK, 128), 1)
    first = lane < ATTN_HEAD_DIM
    ones = jnp.ones((kwin.shape[0], 128), BF16)
    outs = []
    for p in range(ATTN_HEADS // 2):
        sl = slice(p * 128, (p + 1) * 128)
        qp, kp, vp = qs[:, sl], kwin[:, sl], vwin[:, sl]
        zero = jnp.zeros_like(qp)
        q2 = jnp.concatenate([jnp.where(first, qp, zero), jnp.where(first, zero, qp)], axis=0)
        s = lax.dot_general(q2, kp, (((1,), (1,)), ((), ())), preferred_element_type=F32)
        s = s + bias_ref[p, :, lo:]
        e = jnp.exp2(s - jnp.max(s, axis=-1, keepdims=True)).astype(BF16)
        oa = jnp.dot(e, jnp.concatenate([vp, ones], axis=1), preferred_element_type=F32)
        o2 = oa[:, :128] / oa[:, 128:]
        outs.append(jnp.where(first, o2[:ATTN_QBLK], o2[ATTN_QBLK:]))
    o = jnp.concatenate(outs, axis=1)
    o_ref[...] = _rms(o, gain).astype(o_ref.dtype)


def _attn_kernel(q_ref, k_ref, v_ref, bias_ref, gain_ref, o_ref):
    i = pl.program_id(1)
    gain = gain_ref[...]
    left = LEFT_CHUNKS * CHUNK
    short = left // ATTN_QBLK

    def rows(s):
        return pl.ds(s * ATTN_QBLK, ATTN_QBLK)

    @pl.when(i == 0)
    def _():
        for blk in range(ATTN_BLOCKS_PER_STEP):
            if blk < short:
                w = (blk + 1) * ATTN_QBLK
                _attn_block(q_ref[rows(blk), :], k_ref[0:w, :], v_ref[0:w, :], bias_ref, ATTN_WIN - w, gain,
                            o_ref.at[rows(blk), :])
            else:
                start = blk * ATTN_QBLK - left
                _attn_block(q_ref[rows(blk), :], k_ref[start:start + ATTN_WIN, :], v_ref[start:start + ATTN_WIN, :],
                            bias_ref, 0, gain, o_ref.at[rows(blk), :])

    @pl.when(i > 0)
    def _():
        for s in range(ATTN_BLOCKS_PER_STEP):
            start = pl.multiple_of((i * ATTN_BLOCKS_PER_STEP + s) * ATTN_QBLK - left, ATTN_QBLK)
            _attn_block(q_ref[rows(s), :], k_ref[pl.ds(start, ATTN_WIN), :], v_ref[pl.ds(start, ATTN_WIN), :],
                        bias_ref, 0, gain, o_ref.at[rows(s), :])


def _attn_bias(rel_bias):
    q, w = ATTN_QBLK, ATTN_WIN
    period = q + w - 1
    x = np.arange(period)
    x = np.where(x < w, x, x - period)
    vec = rel_bias.astype(F32)[:, np.clip(LEFT_CHUNKS * CHUNK - x, -MAX_REL, MAX_REL) + MAX_REL]
    bias = jnp.tile(vec, (1, q))[:, :q * (period - 1)].reshape(-1, q, period - 1)[:, :, :w]
    qi = np.arange(q)[:, None]
    kj = np.arange(w)[None, :]
    dchunk = qi // CHUNK + LEFT_CHUNKS - kj // CHUNK
    in_band = (dchunk >= 0) & (dchunk <= LEFT_CHUNKS)
    bias = jnp.where(in_band[None], bias * LOG2E, NEG)
    return bias.reshape(-1, 2 * q, w)


def _attn(proj, bias, gain, *, batch, seq):
    n = batch * seq
    w = ATTN_WIDTH
    rows = ATTN_QBLK * ATTN_BLOCKS_PER_STEP
    steps = seq // rows
    qcol = 4 * HGRN_WIDTH // w
    return pl.pallas_call(
        _attn_kernel,
        grid=(batch, steps),
        in_specs=[
            pl.BlockSpec((rows, w), lambda b, i: (b * steps + i, qcol)),
            pl.BlockSpec((seq, w), lambda b, i: (b, qcol + 1)),
            pl.BlockSpec((seq, w), lambda b, i: (b, qcol + 2)),
            pl.BlockSpec((ATTN_HEADS // 2, 2 * ATTN_QBLK, ATTN_WIN), lambda b, i: (0, 0, 0)),
            pl.BlockSpec((1, w), lambda b, i: (0, 0)),
        ],
        out_specs=pl.BlockSpec((rows, w), lambda b, i: (b * steps + i, 0)),
        out_shape=jax.ShapeDtypeStruct((n, w), BF16),
        compiler_params=_params(("parallel", "arbitrary")),
        name="attn",
    )(proj, proj, proj, bias, gain)


def _post_kernel(h_ref, yh_ref, ya_ref, wout_ref, gq_ref, wq_ref, kv_ref, wo_ref, o_ref):
    d = h_ref.shape[1]
    hw = yh_ref.shape[1]
    h1 = (h_ref[...]
          + jnp.dot(yh_ref[...], wout_ref[0:hw, :], preferred_element_type=F32)
          + jnp.dot(ya_ref[...], wout_ref[hw:, :], preferred_element_type=F32))
    hn = _rms(h1, gq_ref[...]).astype(BF16)
    dh = d // MEM_HEADS
    q = (jnp.dot(hn, wq_ref[...], preferred_element_type=F32) * (dh ** -0.5)).astype(BF16)
    outs = []
    for hd in range(MEM_HEADS):
        sl = slice(hd * dh, (hd + 1) * dh)
        k = kv_ref[:, sl]
        v = kv_ref[:, d + hd * dh:d + (hd + 1) * dh]
        s = lax.dot_general(q[:, sl], k, (((1,), (1,)), ((), ())), preferred_element_type=F32)
        e, l = _softmax_rows(s)
        outs.append((jnp.dot(e.astype(BF16), v, preferred_element_type=F32) / l).astype(BF16))
    o = jnp.concatenate(outs, axis=1)
    o_ref[...] = h1 + jnp.dot(o, wo_ref[...], preferred_element_type=F32)


def _post(h, yh, ya, w_out, gq, wq, kv, wo, *, seq, mem_len, tm):
    n, d = h.shape
    per_batch = seq // tm
    const = lambda i: (0, 0)
    resident = dict(pipeline_mode=pl.Buffered(1))
    return pl.pallas_call(
        _post_kernel,
        grid=(n // tm,),
        in_specs=[
            pl.BlockSpec((tm, d), lambda i: (i, 0)),
            pl.BlockSpec((tm, yh.shape[1]), lambda i: (i, 0)),
            pl.BlockSpec((tm, ya.shape[1]), lambda i: (i, 0)),
            pl.BlockSpec(w_out.shape, const, **resident),
            pl.BlockSpec((1, d), const),
            pl.BlockSpec(wq.shape, const, **resident),
            pl.BlockSpec((mem_len, 2 * d), lambda i: (i // per_batch, 0)),
            pl.BlockSpec(wo.shape, const, **resident),
        ],
        out_specs=pl.BlockSpec((tm, d), lambda i: (i, 0)),
        out_shape=jax.ShapeDtypeStruct((n, d), F32),
        compiler_params=_params(("parallel",)),
        name="post",
    )(h, yh, ya, w_out, gq, wq, kv, wo)


def _ffn_kernel(h_ref, g_ref, wg_ref, wu_ref, wd_ref, o_ref, xn_scr, acc_scr, *, tf):
    xn_scr[...] = _rms(h_ref[...], g_ref[...]).astype(BF16)
    acc_scr[...] = jnp.zeros_like(acc_scr)

    def slab(c, carry):
        c0 = pl.multiple_of(c * tf, tf)
        xn = xn_scr[...]
        a = jnp.dot(xn, wg_ref[:, pl.ds(c0, tf)], preferred_element_type=F32)
        u = jnp.dot(xn, wu_ref[:, pl.ds(c0, tf)], preferred_element_type=F32)
        act = (jax.nn.silu(a) * u).astype(BF16)
        acc_scr[...] += jnp.dot(act, wd_ref[pl.ds(c0, tf), :], preferred_element_type=F32)
        return carry

    lax.fori_loop(0, wg_ref.shape[1] // tf, slab, 0)
    o_ref[...] = h_ref[...] + acc_scr[...]


def _ffn(h, gain, wg, wu, wd, *, tm, tf):
    n, d = h.shape
    dff = wg.shape[1]
    resident = dict(pipeline_mode=pl.Buffered(1))
    return pl.pallas_call(
        functools.partial(_ffn_kernel, tf=tf),
        grid=(n // tm,),
        in_specs=[
            pl.BlockSpec((tm, d), lambda i: (i, 0)),
            pl.BlockSpec((1, d), lambda i: (0, 0)),
            pl.BlockSpec((d, dff), lambda i: (0, 0), **resident),
            pl.BlockSpec((d, dff), lambda i: (0, 0), **resident),
            pl.BlockSpec((dff, d), lambda i: (0, 0), **resident),
        ],
        out_specs=pl.BlockSpec((tm, d), lambda i: (i, 0)),
        out_shape=jax.ShapeDtypeStruct((n, d), F32),
        scratch_shapes=[pltpu.VMEM((tm, d), BF16), pltpu.VMEM((tm, d), F32)],
        compiler_params=_params(("parallel",)),
        name="ffn",
    )(h, gain, wg, wu, wd)


def _router_kernel(h_ref, g_ref, w1_ref, w2_ref, tri_ref, ri_ref, rg_ref, cnt_ref, hn_ref, carry_scr):
    i = pl.program_id(0)
    tm = h_ref.shape[0]

    @pl.when(i == 0)
    def _():
        carry_scr[...] = jnp.zeros_like(carry_scr)

    hn = _rms(h_ref[...], g_ref[...])
    for c in range(LANE_TILES):
        hn_ref[pl.ds(c, tm, stride=LANE_TILES), :] = hn[:, c * 128:(c + 1) * 128]
    a1 = hn.astype(BF16)
    a2 = (hn - a1.astype(F32)).astype(BF16)
    lg = (jnp.dot(a1, w1_ref[...], preferred_element_type=F32)
          + jnp.dot(a1, w2_ref[...], preferred_element_type=F32)
          + jnp.dot(a2, w1_ref[...], preferred_element_type=F32))
    lgt = lg.T[0:N_EXPERTS, :]
    ids = lax.broadcasted_iota(jnp.int32, (N_EXPERTS, tm), 0)
    m1 = jnp.max(lgt, axis=0, keepdims=True)
    i1 = jnp.min(jnp.where(lgt == m1, ids, N_EXPERTS), axis=0, keepdims=True)
    rest = jnp.where(ids == i1, -jnp.inf, lgt)
    m2 = jnp.max(rest, axis=0, keepdims=True)
    i2 = jnp.min(jnp.where(rest == m2, ids, N_EXPERTS), axis=0, keepdims=True)
    e = jnp.exp(m2 - m1)
    g1 = 1.0 / (1.0 + e)
    g2 = e * g1
    hit1 = ids == i1
    hit2 = ids == i2
    onehot = jnp.where(hit1 | hit2, 1.0, 0.0)
    before = jnp.dot(onehot.astype(BF16), tri_ref[...], preferred_element_type=F32) + carry_scr[...][:, 0:1]
    r1 = jnp.sum(jnp.where(hit1, before, 0.0), axis=0, keepdims=True)
    r2 = jnp.sum(jnp.where(hit2, before, 0.0), axis=0, keepdims=True)
    carry_scr[...] = carry_scr[...] + jnp.sum(onehot, axis=1, keepdims=True)
    cnt_ref[...] = carry_scr[...]
    row = lax.broadcasted_iota(jnp.int32, (8, tm), 0)
    ri_ref[...] = jnp.where(row == 0, i1, jnp.where(row == 1, i2, jnp.where(
        row == 2, r1.astype(jnp.int32), jnp.where(row == 3, r2.astype(jnp.int32), 0))))
    rg_ref[...] = jnp.where(row == 0, g1, jnp.where(row == 1, g2, 0.0))


def _router(h, gain, w1, w2, tri, *, tm):
    n, d = h.shape
    return pl.pallas_call(
        _router_kernel,
        grid=(n // tm,),
        in_specs=[
            pl.BlockSpec((tm, d), lambda i: (i, 0)),
            pl.BlockSpec((1, d), lambda i: (0, 0)),
            pl.BlockSpec(w1.shape, lambda i: (0, 0)),
            pl.BlockSpec(w2.shape, lambda i: (0, 0)),
            pl.BlockSpec((tm, tm), lambda i: (0, 0)),
        ],
        out_specs=[
            pl.BlockSpec((8, tm), lambda i: (0, i)),
            pl.BlockSpec((8, tm), lambda i: (0, i)),
            pl.BlockSpec((N_EXPERTS, 128), lambda i: (0, 0)),
            pl.BlockSpec((tm * LANE_TILES, 128), lambda i: (i, 0)),
        ],
        out_shape=[
            jax.ShapeDtypeStruct((8, n), jnp.int32),
            jax.ShapeDtypeStruct((8, n), F32),
            jax.ShapeDtypeStruct((N_EXPERTS, 128), F32),
            jax.ShapeDtypeStruct((n * LANE_TILES, 128), F32),
        ],
        scratch_shapes=[pltpu.VMEM((N_EXPERTS, 128), F32)],
        compiler_params=_params(("arbitrary",)),
        name="router",
    )(h, gain, w1, w2, tri)


def _experts_kernel(be_ref, nu_ref, idx_hbm, h_hbm, wg_hbm, wu_hbm, wd_hbm, y_hbm,
                    idx_smem, xbuf, obuf, xb_scr, acc_scr, wgb, wub, wdb, isem, gsem, ssem, wsem,
                    *, n_tokens, li):
    j = pl.program_id(0)
    nu = nu_ref[0]
    rows_per_step = MOE_ROWS // MOE_STEPS
    tf = wgb.shape[2]

    def idx_copy(b):
        s = lax.rem(b + 1, 4)
        return pltpu.make_async_copy(idx_hbm.at[b + 1], idx_smem.at[s], isem.at[s])

    def gather(b, step, u):
        tok = idx_smem[lax.rem(b + 1, 4), step, u]
        s = lax.rem(b + 2, 2)
        r = step * rows_per_step + u
        return pltpu.make_async_copy(h_hbm.at[pl.ds(tok * LANE_TILES, LANE_TILES)],
                                     xbuf.at[s, pl.ds(r * LANE_TILES, LANE_TILES)], gsem.at[s])

    def scatter(b, step, u):
        tgt = idx_smem[lax.rem(b + 1, 4), MOE_STEPS + step, u]
        s = lax.rem(b + 2, 2)
        r = step * rows_per_step + u
        return pltpu.make_async_copy(obuf.at[s, pl.ds(r * LANE_TILES, LANE_TILES)],
                                     y_hbm.at[pl.ds(tgt * LANE_TILES, LANE_TILES)], ssem.at[s])

    def wait_gathers(b):
        s = lax.rem(b + 2, 2)
        pltpu.make_async_copy(h_hbm.at[pl.ds(0, MOE_ROWS * LANE_TILES)], xbuf.at[s], gsem.at[s]).wait()

    def wait_scatters(b):
        s = lax.rem(b + 2, 2)
        pltpu.make_async_copy(obuf.at[s], y_hbm.at[pl.ds(0, MOE_ROWS * LANE_TILES)], ssem.at[s]).wait()

    def weight_copies(b, f):
        e = be_ref[b]
        s = lax.rem(b * MOE_STEPS + f, 2)
        c0 = pl.multiple_of(f * tf, tf)
        return (pltpu.make_async_copy(wg_hbm.at[li, e, :, pl.ds(c0, tf)], wgb.at[s], wsem.at[s]),
                pltpu.make_async_copy(wu_hbm.at[li, e, :, pl.ds(c0, tf)], wub.at[s], wsem.at[s]),
                pltpu.make_async_copy(wd_hbm.at[li, e, pl.ds(c0, tf), :], wdb.at[s], wsem.at[s]))

    def for_rows(fn, steps=range(MOE_STEPS)):
        for step in steps:
            def body(u, carry, step=step):
                fn(step, u)
                return carry
            lax.fori_loop(0, rows_per_step, body, 0, unroll=16)

    @pl.when(j == 0)
    def _():
        obuf[1] = jnp.zeros(obuf.shape[1:], obuf.dtype)
        for half in range(2):
            spare = pltpu.make_async_copy(
                obuf.at[1],
                y_hbm.at[pl.ds((TOP_K * n_tokens + half * MOE_ROWS) * LANE_TILES, MOE_ROWS * LANE_TILES)],
                ssem.at[half])
            spare.start()
            spare.wait()
        for b in (-1, 0, 1):
            idx_copy(b).start()
        idx_copy(-1).wait()
        idx_copy(0).wait()
        for c in weight_copies(0, 0):
            c.start()
        for_rows(lambda step, u: gather(0, step, u).start())

    @pl.when(j <= nu)
    def _():
        @pl.when(j == nu)
        def _():
            wait_gathers(j)

        @pl.when(j >= 1)
        def _():
            wait_scatters(j - 2)

        idx_copy(j + 1).wait()

        @pl.when(j < nu)
        def _():
            idx_copy(j + 2).start()

    @pl.when(j < nu)
    def _():
        slot = lax.rem(j, 2)
        acc_scr[...] = jnp.zeros_like(acc_scr)
        wait_gathers(j)
        for c in range(LANE_TILES):
            xb_scr[:, c * 128:(c + 1) * 128] = (
                xbuf[slot, pl.ds(c, MOE_ROWS, stride=LANE_TILES), :].astype(BF16))

        def slab(f, carry):
            for c in weight_copies(j, f):
                c.wait()
            last = f == MOE_STEPS - 1
            nb = jnp.where(last, j + 1, j)
            nf_ = jnp.where(last, 0, f + 1)

            @pl.when(nb < nu)
            def _():
                for c in weight_copies(nb, nf_):
                    c.start()

            ws = lax.rem(j * MOE_STEPS + f, 2)
            for u in range(rows_per_step):
                gather(j + 1, f, u).start()
                scatter(j - 1, f, u).start(priority=u % 2)
            xb = xb_scr[...]
            a = jnp.dot(xb, wgb[ws].astype(BF16), preferred_element_type=F32)
            u_ = jnp.dot(xb, wub[ws].astype(BF16), preferred_element_type=F32)
            act = (jax.nn.silu(a) * u_).astype(BF16)
            acc_scr[...] += jnp.dot(act, wdb[ws].astype(BF16), preferred_element_type=F32)
            return carry

        lax.fori_loop(0, MOE_STEPS, slab, 0)
        for c in range(LANE_TILES):
            obuf[slot, pl.ds(c, MOE_ROWS, stride=LANE_TILES), :] = acc_scr[:, c * 128:(c + 1) * 128]

    @pl.when(j == nu)
    def _():
        for_rows(lambda step, u: scatter(j - 1, step, u).start())
        wait_scatters(j - 1)


def _experts(block_expert, n_used, idx, h3, wg, wu, wd, li):
    n = h3.shape[0] // LANE_TILES
    d = LANE_TILES * 128
    dff = wg.shape[3]
    tf = dff // MOE_STEPS
    n_blocks = idx.shape[0] - 3
    hbm = pl.BlockSpec(memory_space=pl.ANY)
    grid_spec = pltpu.PrefetchScalarGridSpec(
        num_scalar_prefetch=2,
        grid=(n_blocks + 1,),
        in_specs=[hbm, hbm, hbm, hbm, hbm],
        out_specs=hbm,
        scratch_shapes=[
            pltpu.SMEM((4, 2 * MOE_STEPS, MOE_ROWS // MOE_STEPS), jnp.int32),
            pltpu.VMEM((2, MOE_ROWS * LANE_TILES, 128), F32),
            pltpu.VMEM((2, MOE_ROWS * LANE_TILES, 128), F32),
            pltpu.VMEM((MOE_ROWS, d), BF16),
            pltpu.VMEM((MOE_ROWS, d), F32),
            pltpu.VMEM((2, d, tf), F32),
            pltpu.VMEM((2, d, tf), F32),
            pltpu.VMEM((2, tf, d), F32),
            pltpu.SemaphoreType.DMA((4,)),
            pltpu.SemaphoreType.DMA((2,)),
            pltpu.SemaphoreType.DMA((2,)),
            pltpu.SemaphoreType.DMA((2,)),
        ],
    )
    return pl.pallas_call(
        functools.partial(_experts_kernel, n_tokens=n, li=li),
        grid_spec=grid_spec,
        out_shape=jax.ShapeDtypeStruct(((TOP_K * n + 2 * MOE_ROWS) * LANE_TILES, 128), F32),
        compiler_params=_params(("arbitrary",)),
        name="experts",
    )(block_expert, n_used, idx, h3, wg, wu, wd)


def _combine_kernel(h_ref, rg_ref, gf_ref, y0_ref, y1_ref, o_ref, *, final_norm):
    gates = rg_ref[...].T
    g0, g1 = gates[:, 0:1], gates[:, 1:2]
    tm = h_ref.shape[0]
    out = jnp.concatenate([g0 * y0_ref[pl.ds(c, tm, stride=LANE_TILES), :]
                           + g1 * y1_ref[pl.ds(c, tm, stride=LANE_TILES), :]
                           for c in range(LANE_TILES)], axis=1)
    out = h_ref[...] + out
    if final_norm:
        out = _rms(out, gf_ref[...])
    o_ref[...] = out


def _combine(h, rg, gain_final, y, *, tm, final_norm):
    n, d = h.shape
    return pl.pallas_call(
        functools.partial(_combine_kernel, final_norm=final_norm),
        grid=(n // tm,),
        in_specs=[
            pl.BlockSpec((tm, d), lambda i: (i, 0)),
            pl.BlockSpec((8, tm), lambda i: (0, i)),
            pl.BlockSpec((1, d), lambda i: (0, 0)),
            pl.BlockSpec((tm * LANE_TILES, 128), lambda i: (i, 0)),
            pl.BlockSpec((tm * LANE_TILES, 128), lambda i: (i + n // tm, 0)),
        ],
        out_specs=pl.BlockSpec((tm, d), lambda i: (i, 0)),
        out_shape=jax.ShapeDtypeStruct((n, d), F32),
        compiler_params=_params(("parallel",)),
        name="combine",
    )(h, rg, gain_final, y, y)


def _moe(h, gain, w_router, wg, wu, wd, li, gain_final, *, final_norm, tm_route, tm_rows):
    n, d = h.shape
    wr = jnp.zeros((d, 128), F32).at[:, :N_EXPERTS].set(w_router.astype(F32))
    w1 = wr.astype(BF16)
    w2 = (wr - w1.astype(F32)).astype(BF16)
    tri = jnp.asarray(np.triu(np.ones((tm_route, tm_route), np.float32), 1), BF16)
    ri, rg, cnt, hn3 = _router(h, gain, w1, w2, tri, tm=tm_route)

    counts = cnt[:, 0].astype(jnp.int32)
    padded = (counts + MOE_ROWS - 1) // MOE_ROWS * MOE_ROWS
    padded_end = jnp.cumsum(padded)
    padded_start = padded_end - padded
    n_blocks = -(-(n * TOP_K + N_EXPERTS * (MOE_ROWS - 1)) // MOE_ROWS)
    block_start = jnp.arange(n_blocks, dtype=jnp.int32) * MOE_ROWS
    block_expert = jnp.minimum(
        jnp.sum(block_start[:, None] >= padded_end[None, :], axis=1), N_EXPERTS - 1).astype(jnp.int32)
    n_used = (padded_end[-1:] // MOE_ROWS).astype(jnp.int32)
    onehot = ri[0:TOP_K, :, None] == jnp.arange(N_EXPERTS, dtype=jnp.int32)
    dest = jnp.sum(jnp.where(onehot, padded_start, 0), axis=-1) + ri[TOP_K:2 * TOP_K]

    n_idx = (n_blocks + 3) * MOE_ROWS
    pos = jnp.arange(n_idx, dtype=jnp.int32)
    spare = TOP_K * n + (pos // MOE_ROWS % 2) * MOE_ROWS + pos % MOE_ROWS
    hit = jnp.zeros((n_idx,), jnp.int32).at[dest.reshape(-1) + MOE_ROWS].add(
        jnp.arange(1, TOP_K * n + 1, dtype=jnp.int32), unique_indices=True)
    target = jnp.where(hit > 0, hit - 1, spare)
    token = jnp.where(target < TOP_K * n, target % n, 0)
    per_step = MOE_ROWS // MOE_STEPS
    idx = jnp.concatenate([token.reshape(-1, MOE_STEPS, per_step),
                           target.reshape(-1, MOE_STEPS, per_step)], axis=1)

    y = _experts(block_expert, n_used, idx, hn3, wg, wu, wd, li)
    return _combine(h, rg, gain_final, y, tm=tm_rows, final_norm=final_norm)


def kernel(x, mem, w_in, hgrn_lower_bound, hgrn_out_gain, attn_rel_bias, attn_out_gain, w_out,
           norm_mix, norm_mem_q, norm_mem_kv, w_mem_q, w_mem_kv, w_mem_o, norm_ffn,
           w_ffn_gate, w_ffn_up, w_ffn_down, w_router, w_exp_gate, w_exp_up, w_exp_down, norm_final):
    batch, seq, d = x.shape
    mem_len = mem.shape[1]
    depth = w_in.shape[0]
    n = batch * seq
    assert seq % ROWS_POST == 0 and seq >= ATTN_WIN and n % max(ROWS_IN_PROJ, ROWS_FFN) == 0
    assert batch % HGRN_PAIR == 0

    lb_probs = jax.nn.softmax(hgrn_lower_bound.astype(F32), axis=0)
    lower_bounds = jnp.cumsum(lb_probs, axis=0) - lb_probs[0]

    h = x.reshape(n, d)
    mem2 = mem.reshape(batch * mem_len, d)
    row = lambda v: v.reshape(1, -1).astype(F32)
    for layer in range(depth):
        proj = _norm_matmul(h, row(norm_mix[layer]), w_in[layer].astype(BF16),
                            tm=ROWS_IN_PROJ, col_chunk=PROJ_COLS, name="in_proj")
        y_hgrn = _hgrn(proj, row(lower_bounds[layer]), row(hgrn_out_gain[layer]),
                       batch=batch, seq=seq, ts=ROWS_HGRN)
        y_attn = _attn(proj, _attn_bias(attn_rel_bias[layer]), row(attn_out_gain[layer]),
                       batch=batch, seq=seq)
        kv = _norm_matmul(mem2, row(norm_mem_kv[layer]), w_mem_kv[layer].astype(BF16),
                          tm=min(ROWS_MEM_KV, batch * mem_len), col_chunk=PROJ_COLS, name="mem_kv")
        h = _post(h, y_hgrn, y_attn, w_out[layer].astype(BF16), row(norm_mem_q[layer]),
                  w_mem_q[layer].astype(BF16), kv, w_mem_o[layer].astype(BF16),
                  seq=seq, mem_len=mem_len, tm=ROWS_POST)
        j = layer // 2
        if layer % 2 == 0:
            h = _ffn(h, row(norm_ffn[layer]), w_ffn_gate[j].astype(BF16), w_ffn_up[j].astype(BF16),
                     w_ffn_down[j].astype(BF16), tm=ROWS_FFN, tf=FFN_COLS)
        else:
            last = layer == depth - 1
            h = _moe(h, row(norm_ffn[layer]), w_router[j], w_exp_gate, w_exp_up, w_exp_down, j, row(norm_final),
                     final_norm=last, tm_route=ROWS_ROUTER, tm_rows=ROWS_COMBINE)
    if depth % 2 == 1:
        h = _norm_only(h, row(norm_final))
    return h.reshape(batch, seq, d)


def _norm_only_kernel(x_ref, g_ref, o_ref):
    o_ref[...] = _rms(x_ref[...], g_ref[...])


def _norm_only(x, gain, tm=1024):
    n, d = x.shape
    return pl.pallas_call(
        _norm_only_kernel,
        grid=(n // tm,),
        in_specs=[pl.BlockSpec((tm, d), lambda i: (i, 0)), pl.BlockSpec((1, d), lambda i: (0, 0))],
        out_specs=pl.BlockSpec((tm, d), lambda i: (i, 0)),
        out_shape=jax.ShapeDtypeStruct((n, d), F32),
        compiler_params=_params(("parallel",)),
        name="final_norm",
    )(x, gain)
```
